```python
import jax, jax.numpy as jnp
from jax import lax
import numpy as np

D_MODEL = 1024
BATCH = 8
SEQ = 4096
DEPTH = 2

GRID_W = 64
CTX_LEN = 256

N_Q_HEADS = 8
N_KV_HEADS = 2
GQA_GROUP = N_Q_HEADS // N_KV_HEADS
HEAD_DIM = 64
ATTN_W = N_Q_HEADS * HEAD_DIM
KV_W = N_KV_HEADS * HEAD_DIM
ROPE_THETA = 10000.0
Q_BLOCK = 128
ATTN_SCALE = HEAD_DIM ** -0.5
CHUNK = 128
SG_GROUPS = 4
SG_GROUP_W = 128
SG_W = SG_GROUPS * SG_GROUP_W
FT_GROUPS = 4
FT_GROUP_W = 128
FT_W = FT_GROUPS * FT_GROUP_W
N_BRANCH = 3
BRANCH_W = 512
OFF_Q = 0
OFF_K = OFF_Q + ATTN_W
OFF_V = OFF_K + KV_W
OFF_U = OFF_V + KV_W
OFF_SGV = OFF_U + SG_W
OFF_FT = OFF_SGV + SG_W
OFF_GATE = OFF_FT + FT_W
IN_W = OFF_GATE + N_BRANCH * D_MODEL
D_FF = 2816
N_EXPERTS = 8
TOP_K = 2
D_FF_EXPERT = 3584
N_DENSE = (DEPTH + 1) // 2
N_MOE = DEPTH // 2
DEEPNORM_ALPHA = (2 * DEPTH) ** 0.25
DEEPNORM_BETA = (8 * DEPTH) ** -0.25
LN_EPS = 1e-6
RMS_EPS = 1e-6

kernel_name = "hybrid_gated_attn_sgmlp_fourier_moe_dit"


def _layer_norm(x):
    xf = x.astype(jnp.float32)
    mu = jnp.mean(xf, axis=-1, keepdims=True)
    var = jnp.mean(jnp.square(xf - mu), axis=-1, keepdims=True)
    return ((xf - mu) * lax.rsqrt(var + LN_EPS)).astype(x.dtype)


def _post_norm(x, y, g, b):
    r = DEEPNORM_ALPHA * x.astype(jnp.float32) + y.astype(jnp.float32)
    mu = jnp.mean(r, axis=-1, keepdims=True)
    var = jnp.mean(jnp.square(r - mu), axis=-1, keepdims=True)
    return ((r - mu) * lax.rsqrt(var + LN_EPS) * g + b).astype(x.dtype)


def _rms_norm(x, g):
    xf = x.astype(jnp.float32)
    return (xf * lax.rsqrt(jnp.mean(jnp.square(xf), axis=-1, keepdims=True) + RMS_EPS) * g).astype(x.dtype)


def _modulation(cond, w, b):
    m = jax.nn.silu(cond) @ w + b
    return [mi[..., None, :] for mi in jnp.split(m, 6, axis=-1)]


def _modulate(x, shift, scale):
    return _layer_norm(x) * (1 + scale) + shift


def _split_heads(z, n):
    return z.reshape(*z.shape[:-1], n, HEAD_DIM)


def _rope_1d(x, pos):
    d2 = x.shape[-1] // 2
    inv = ROPE_THETA ** (-jnp.arange(d2, dtype=jnp.float32) / d2)
    ang = pos.astype(jnp.float32)[:, None] * inv[None, :]
    cos = jnp.cos(ang)[:, None, :]
    sin = jnp.sin(ang)[:, None, :]
    xf = x.astype(jnp.float32)
    x1, x2 = xf[..., :d2], xf[..., d2:]
    return jnp.concatenate([x1 * cos - x2 * sin, x2 * cos + x1 * sin], axis=-1).astype(x.dtype)


def _axial_rope(x, t_row, t_col):
    half = HEAD_DIM // 2
    return jnp.concatenate([_rope_1d(x[..., :half], t_row), _rope_1d(x[..., half:], t_col)], axis=-1)


def _gqa_attend(q, k, v):
    s = jnp.einsum('bqkgd,btkd->bkgqt', q, k).astype(jnp.float32) * ATTN_SCALE
    p = jax.nn.softmax(s, axis=-1).astype(v.dtype)
    return jnp.einsum('bkgqt,btkd->bqkgd', p, v)


def _latent_attention(q, k_all, v_all):
    b, s = q.shape[:2]
    nb = s // Q_BLOCK
    qb = jnp.moveaxis(q.reshape(b, nb, Q_BLOCK, N_KV_HEADS, GQA_GROUP, HEAD_DIM), 1, 0)
    ob = lax.map(lambda blk: _gqa_attend(blk, k_all, v_all), qb)
    return jnp.moveaxis(ob, 0, 1).reshape(b, s, ATTN_W)


def _chunk_gating(u, v, w_s, b_s):
    b, l, _ = u.shape
    vg = _layer_norm(v.reshape(b, l, SG_GROUPS, SG_GROUP_W))
    vg = vg.reshape(b, l // CHUNK, CHUNK, SG_GROUPS, SG_GROUP_W)
    mixed = jnp.einsum('gpq,bnqgd->bnpgd', w_s, vg) + jnp.transpose(b_s)[:, :, None]
    return u * mixed.reshape(b, l, SG_W)


def _fourier_mix(f):
    b, l, _ = f.shape
    fg = f.reshape(b, l, FT_GROUPS, FT_GROUP_W).astype(jnp.float32)
    out = jnp.fft.fft2(fg, axes=(1, 3), norm='ortho').real
    return out.astype(f.dtype).reshape(b, l, FT_W)


def _merge_branches(z, y_attn, w_s, b_s, w_branch, w_out):
    y_sg = _chunk_gating(jax.nn.gelu(z[..., OFF_U:OFF_SGV]), jax.nn.gelu(z[..., OFF_SGV:OFF_FT]), w_s, b_s)
    y_ft = _fourier_mix(z[..., OFF_FT:OFF_GATE])
    y = jnp.stack([y_attn, y_sg, y_ft], axis=-2)
    p = jnp.einsum('blnc,ncd->blnd', y, w_branch)
    g = jax.nn.sigmoid(z[..., OFF_GATE:].astype(jnp.float32)).astype(z.dtype)
    g = g.reshape(*z.shape[:-1], N_BRANCH, D_MODEL)
    return jnp.sum(g * p, axis=-2) @ w_out


def _token_mixer(h_lat, h_ctx, w_in, q_g, k_g, w_s, b_s, w_branch, w_out, t_row, t_col, with_ctx_out):
    b, s, _ = h_lat.shape
    z = h_lat @ w_in
    q = _axial_rope(_rms_norm(_split_heads(z[..., OFF_Q:OFF_K], N_Q_HEADS), q_g), t_row, t_col)
    k = _axial_rope(_rms_norm(_split_heads(z[..., OFF_K:OFF_V], N_KV_HEADS), k_g), t_row, t_col)
    v = _split_heads(z[..., OFF_V:OFF_U], N_KV_HEADS)
    zc = h_ctx @ w_in if with_ctx_out else h_ctx @ w_in[:, OFF_K:OFF_U]
    base = 0 if with_ctx_out else OFF_K
    k_c = _rms_norm(_split_heads(zc[..., OFF_K - base:OFF_V - base], N_KV_HEADS), k_g)
    v_c = _split_heads(zc[..., OFF_V - base:OFF_U - base], N_KV_HEADS)
    k_all = jnp.concatenate([k_c, k], axis=1)
    v_all = jnp.concatenate([v_c, v], axis=1)
    y_lat = _merge_branches(z, _latent_attention(q, k_all, v_all), w_s, b_s, w_branch, w_out)
    y_ctx = None
    if with_ctx_out:
        cl = h_ctx.shape[1]
        q_c = _rms_norm(_split_heads(zc[..., OFF_Q:OFF_K], N_Q_HEADS), q_g)
        q_c = q_c.reshape(b, cl, N_KV_HEADS, GQA_GROUP, HEAD_DIM)
        y_attn_c = _gqa_attend(q_c, k_c, v_c).reshape(b, cl, ATTN_W)
        y_ctx = _merge_branches(zc, y_attn_c, w_s, b_s, w_branch, w_out)
    return y_lat, y_ctx


def _swiglu(x, wg, wu, wd):
    return (jax.nn.silu(x @ wg) * (x @ wu)) @ wd


def _moe_swiglu(x, router, wg, wu, wd):
    logits = (x @ router).astype(jnp.float32)
    top_v, top_i = lax.top_k(logits, TOP_K)
    top_w = jax.nn.softmax(top_v, axis=-1)
    gates = jnp.einsum('blk,blke->ble', top_w, jax.nn.one_hot(top_i, N_EXPERTS, dtype=jnp.float32)).astype(x.dtype)
    out = jnp.zeros_like(x)
    for e in range(N_EXPERTS):
        out = out + gates[..., e:e + 1] * _swiglu(x, wg[e], wu[e], wd[e])
    return out


def setup_inputs(seed: int = 0) -> dict:
    key = jax.random.key(seed)
    ks = jax.random.split(key, 24)

    def nrm(k, shape, scale):
        return jax.random.normal(k, shape, jnp.float32) * scale

    d = D_MODEL
    col_scale = jnp.ones((IN_W,), jnp.float32).at[OFF_V:OFF_V + KV_W].set(DEEPNORM_BETA)
    return {
        'x': nrm(ks[0], (BATCH, SEQ, d), 1.0),
        'c': nrm(ks[1], (BATCH, d), 1.0),
        'ctx': nrm(ks[2], (BATCH, CTX_LEN, d), 1.0),
        'c_ctx': nrm(ks[3], (d,), 1.0),
        'w_mod': nrm(ks[4], (DEPTH, d, 6 * d), d ** -0.5),
        'b_mod': nrm(ks[5], (DEPTH, 6 * d), 0.02),
        'w_in': nrm(ks[6], (DEPTH, d, IN_W), d ** -0.5) * col_scale,
        'q_norm': 1.0 + nrm(ks[7], (DEPTH, HEAD_DIM), 0.02),
        'k_norm': 1.0 + nrm(ks[8], (DEPTH, HEAD_DIM), 0.02),
        'sg_w': nrm(ks[9], (DEPTH, SG_GROUPS, CHUNK, CHUNK), CHUNK ** -0.5),
        'sg_b': 1.0 + nrm(ks[10], (DEPTH, SG_GROUPS, CHUNK), 0.02),
        'w_branch': nrm(ks[11], (DEPTH, N_BRANCH, BRANCH_W, d), BRANCH_W ** -0.5),
        'w_out': nrm(ks[12], (DEPTH, d, d), d ** -0.5 * DEEPNORM_BETA),
        'ln1_g': 1.0 + nrm(ks[13], (DEPTH, d), 0.02),
        'ln1_b': nrm(ks[14], (DEPTH, d), 0.02),
        'ln2_g': 1.0 + nrm(ks[15], (DEPTH, d), 0.02),
        'ln2_b': nrm(ks[16], (DEPTH, d), 0.02),
        'ffn_w_gate': nrm(ks[17], (N_DENSE, d, D_FF), d ** -0.5),
        'ffn_w_up': nrm(ks[18], (N_DENSE, d, D_FF), d ** -0.5 * DEEPNORM_BETA),
        'ffn_w_down': nrm(ks[19], (N_DENSE, D_FF, d), D_FF ** -0.5 * DEEPNORM_BETA),
        'router': nrm(ks[20], (N_MOE, d, N_EXPERTS), d ** -0.5),
        'exp_w_gate': nrm(ks[21], (N_MOE, N_EXPERTS, d, D_FF_EXPERT), d ** -0.5),
        'exp_w_up': nrm(ks[22], (N_MOE, N_EXPERTS, d, D_FF_EXPERT), d ** -0.5 * DEEPNORM_BETA),
        'exp_w_down': nrm(ks[23], (N_MOE, N_EXPERTS, D_FF_EXPERT, d), D_FF_EXPERT ** -0.5 * DEEPNORM_BETA),
    }


def reference(x, c, ctx, c_ctx, w_mod, b_mod, w_in, q_norm, k_norm, sg_w, sg_b, w_branch, w_out,
              ln1_g, ln1_b, ln2_g, ln2_b, ffn_w_gate, ffn_w_up, ffn_w_down,
              router, exp_w_gate, exp_w_up, exp_w_down):
    s = x.shape[1]
    rows = s // GRID_W
    t_row = jnp.repeat(jnp.arange(rows, dtype=jnp.int32), GRID_W)
    t_col = jnp.tile(jnp.arange(GRID_W, dtype=jnp.int32), rows)
    x_lat, x_ctx = x, ctx
    for l in range(DEPTH):
        last = l == DEPTH - 1
        m_lat = _modulation(c, w_mod[l], b_mod[l])
        m_ctx = _modulation(c_ctx, w_mod[l], b_mod[l])
        h_lat = _modulate(x_lat, m_lat[0], m_lat[1])
        h_ctx = _modulate(x_ctx, m_ctx[0], m_ctx[1])
        y_lat, y_ctx = _token_mixer(h_lat, h_ctx, w_in[l], q_norm[l], k_norm[l], sg_w[l], sg_b[l],
                                    w_branch[l], w_out[l], t_row, t_col, not last)
        x_lat = _post_norm(x_lat, m_lat[2] * y_lat, ln1_g[l], ln1_b[l])
        if not last:
            x_ctx = _post_norm(x_ctx, m_ctx[2] * y_ctx, ln1_g[l], ln1_b[l])
        h_lat = _modulate(x_lat, m_lat[3], m_lat[4])
        if l % 2 == 0:
            i = l // 2
            f_lat = _swiglu(h_lat, ffn_w_gate[i], ffn_w_up[i], ffn_w_down[i])
            if not last:
                f_ctx = _swiglu(_modulate(x_ctx, m_ctx[3], m_ctx[4]), ffn_w_gate[i], ffn_w_up[i], ffn_w_down[i])
        else:
            i = l // 2
            f_lat = _moe_swiglu(h_lat, router[i], exp_w_gate[i], exp_w_up[i], exp_w_down[i])
            if not last:
                f_ctx = _moe_swiglu(_modulate(x_ctx, m_ctx[3], m_ctx[4]), router[i], exp_w_gate[i], exp_w_up[i], exp_w_down[i])
        x_lat = _post_norm(x_lat, m_lat[5] * f_lat, ln2_g[l], ln2_b[l])
        if not last:
            x_ctx = _post_norm(x_ctx, m_ctx[5] * f_ctx, ln2_g[l], ln2_b[l])
    return x_lat
```

```python
import functools
import math

import numpy as np
import jax
import jax.numpy as jnp
from jax import lax
from jax.experimental import pallas as pl
from jax.experimental.pallas import tpu as pltpu

F32 = jnp.float32
BF16 = jnp.bfloat16
U32 = jnp.uint32
I32 = jnp.int32

N_Q_HEADS = 8
N_KV_HEADS = 2
GQA_GROUP = N_Q_HEADS // N_KV_HEADS
HEAD_DIM = 64
ATTN_W = N_Q_HEADS * HEAD_DIM
KV_W = N_KV_HEADS * HEAD_DIM
GRID_W = 64
ROPE_THETA = 10000.0
CHUNK = 128
N_GROUPS = 4
GROUP_W = 128
BRANCH_W = N_GROUPS * GROUP_W
N_BRANCH = 3
OFF_Q = 0
OFF_K = OFF_Q + ATTN_W
OFF_V = OFF_K + KV_W
OFF_U = OFF_V + KV_W
OFF_SGV = OFF_U + BRANCH_W
OFF_FT = OFF_SGV + BRANCH_W
OFF_GATE = OFF_FT + BRANCH_W
LN_EPS = 1e-6
RMS_EPS = 1e-6
GELU_C = math.sqrt(2.0 / math.pi)

LANES = 128
V7X_VMEM_BYTES = 64 * 1024 * 1024
VMEM_LIMIT = V7X_VMEM_BYTES - 8 * 1024 * 1024
FFT_L2 = 64
MOD_ROWS = 16


def _cparams(sem):
    return pltpu.CompilerParams(dimension_semantics=sem, vmem_limit_bytes=VMEM_LIMIT)


def _const_spec(shape):
    nd = len(shape)
    return pl.BlockSpec(shape, lambda *_: (0,) * nd, pipeline_mode=pl.Buffered(1))


def _mod_spec(d, mod_row):
    if mod_row is None:
        return pl.BlockSpec((None, 1, 6 * d), lambda bi, i: (bi, 0, 0))
    return pl.BlockSpec((None, 1, 6 * d), lambda bi, i: (mod_row, 0, 0))


def _dot(a, b):
    return jnp.dot(a, b, preferred_element_type=F32)


def _dot_nt(a, b):
    return lax.dot_general(a, b, (((1,), (1,)), ((), ())), preferred_element_type=F32)


def _sigmoid(x):
    return 0.5 * jnp.tanh(0.5 * x) + 0.5


def _gelu_tanh(x):
    return 0.5 * x * (1.0 + jnp.tanh(GELU_C * (x + 0.044715 * (x * x * x))))


def _normalize(x):
    mu = jnp.mean(x, axis=-1, keepdims=True)
    xc = x - mu
    var = jnp.mean(xc * xc, axis=-1, keepdims=True)
    return xc * lax.rsqrt(var + LN_EPS)


def _split_bf16(x):
    hi = x.astype(BF16)
    lo = (x - hi.astype(F32)).astype(BF16)
    return hi, lo


def _pack_rows(x):
    n = x.shape[1] // 2
    hi = lax.bitcast_convert_type(x[:, :n].astype(BF16).astype(F32), U32)
    lo = lax.bitcast_convert_type(x[:, n:].astype(BF16).astype(F32), U32)
    return hi | (lo >> 16)


def _unpack_rows(p):
    hi = lax.bitcast_convert_type(p & jnp.uint32(0xFFFF0000), F32)
    lo = lax.bitcast_convert_type(p << 16, F32)
    return jnp.concatenate([hi, lo], axis=1)


def _mod_kernel(c_ref, w_ref, b_ref, o_ref):
    c = c_ref[...]
    s = c * _sigmoid(c)
    o_ref[...] = _dot(s.astype(BF16), w_ref[...].astype(BF16)) + b_ref[...]


def _modulation(cond, w_mod, b_mod):
    depth, d, d6 = w_mod.shape
    tn = 2 * d if d6 % (2 * d) == 0 else d
    return pl.pallas_call(
        _mod_kernel,
        grid=(depth, d6 // tn),
        in_specs=[
            pl.BlockSpec((MOD_ROWS, d), lambda l, j: (0, 0)),
            pl.BlockSpec((None, d, tn), lambda l, j: (l, 0, j)),
            pl.BlockSpec((None, 1, tn), lambda l, j: (l, 0, j)),
        ],
        out_specs=pl.BlockSpec((None, MOD_ROWS, tn), lambda l, j: (l, 0, j)),
        out_shape=jax.ShapeDtypeStruct((depth, MOD_ROWS, d6), F32),
        compiler_params=_cparams(("parallel", "parallel")),
        name="modulation",
    )(cond, w_mod, b_mod.reshape(depth, 1, d6))


def _rope_tables(pos_row, pos_col):
    half = HEAD_DIM // 2
    d2 = half // 2
    lane = np.arange(LANES) % HEAD_DIM
    inv = ROPE_THETA ** (-(lane % d2).astype(np.float64) / d2)
    pos = np.where((lane < half)[None, :], pos_row[:, None], pos_col[:, None]).astype(np.float64)
    ang = pos * inv[None, :]
    first = ((lane % half) < d2)[None, :]
    cos = np.cos(ang)
    sin = np.sin(ang)
    sin_a = np.where(first, -sin, 0.0)
    sin_b = np.where(first, 0.0, sin)
    return (jnp.asarray(cos, F32), jnp.asarray(sin_a, F32), jnp.asarray(sin_b, F32))


def _rope(x, cos, sin_a, sin_b):
    d2 = HEAD_DIM // 4
    return x * cos + pltpu.roll(x, LANES - d2, 1) * sin_a + pltpu.roll(x, d2, 1) * sin_b


def _head_rms(z, gmat, gain):
    hi, lo = _split_bf16(z * z)
    ss = _dot(hi, gmat) + _dot(lo, gmat)
    return z * lax.rsqrt(ss * (1.0 / HEAD_DIM) + RMS_EPS) * gain


def _inproj_kernel(x_ref, m_ref, w_ref, cos_ref, sa_ref, sb_ref, qg_ref, kg_ref, gm_ref, sgw_ref, sgb_ref,
                   dft_ref, *out_refs, d, kv_only):
    tm = x_ref.shape[0]
    h = (_normalize(x_ref[...]) * (1.0 + m_ref[:, d:2 * d]) + m_ref[:, 0:d]).astype(BF16)
    cos, sin_a, sin_b = cos_ref[...], sa_ref[...], sb_ref[...]

    def proj(a, b):
        return _dot(h, w_ref[:, a:b])

    if kv_only:
        kt_ref, v_ref = out_refs
    else:
        q_ref, kt_ref, v_ref, ysg_ref, ft_ref, gate_ref = out_refs
        qn = _head_rms(proj(OFF_Q, OFF_K), gm_ref[...], qg_ref[...])
        for s in range(ATTN_W // LANES):
            sl = slice(s * LANES, (s + 1) * LANES)
            q_ref[:, sl] = _rope(qn[:, sl], cos, sin_a, sin_b).astype(BF16)

    kn = _head_rms(proj(OFF_K, OFF_V), gm_ref[0:KV_W, 0:KV_W], kg_ref[...])
    kt_ref[...] = _rope(kn, cos, sin_a, sin_b).T.astype(BF16)
    v_ref[...] = proj(OFF_V, OFF_U).astype(BF16)
    if kv_only:
        return

    u = _gelu_tanh(proj(OFF_U, OFF_SGV))
    v2 = _gelu_tanh(proj(OFF_SGV, OFF_FT))
    for g in range(N_GROUPS):
        gl = slice(g * GROUP_W, (g + 1) * GROUP_W)
        vg = _normalize(v2[:, gl]).astype(BF16)
        for c in range(tm // CHUNK):
            rows = slice(c * CHUNK, (c + 1) * CHUNK)
            mixed = _dot(sgw_ref[g], vg[rows, :]) + sgb_ref[g]
            ysg_ref[rows, gl] = (u[rows, gl] * mixed).astype(BF16)

    zf = proj(OFF_FT, OFF_GATE).astype(BF16)
    for g in range(N_GROUPS):
        cs = _dot(zf[:, g * GROUP_W:(g + 1) * GROUP_W], dft_ref[...])
        ft_ref[:, g * GROUP_W:(g + 1) * GROUP_W] = cs[:, :GROUP_W].astype(BF16)
        ft_ref[:, BRANCH_W + g * GROUP_W:BRANCH_W + (g + 1) * GROUP_W] = cs[:, GROUP_W:].astype(BF16)

    for n in range(N_BRANCH):
        gate_ref[:, n * d:(n + 1) * d] = _sigmoid(proj(OFF_GATE + n * d, OFF_GATE + (n + 1) * d)).astype(BF16)


def _in_projection(x, mod, mod_row, w_in, tables, qg, kg, gmat, sgw, sgb, dftc, *, tm, kv_only):
    b, l, d = x.shape
    in_w = w_in.shape[1]
    cos, sin_a, sin_b = tables
    tok = lambda w: pl.BlockSpec((None, tm, w), lambda bi, i: (bi, i, 0))
    tab = pl.BlockSpec((tm, LANES), lambda bi, i: (i, 0))
    out_specs = [pl.BlockSpec((None, KV_W, tm), lambda bi, i: (bi, 0, i)), tok(KV_W)]
    out_shape = [jax.ShapeDtypeStruct((b, KV_W, l), BF16), jax.ShapeDtypeStruct((b, l, KV_W), BF16)]
    if not kv_only:
        out_specs = [tok(ATTN_W)] + out_specs + [tok(BRANCH_W), tok(2 * BRANCH_W), tok(N_BRANCH * d)]
        out_shape = ([jax.ShapeDtypeStruct((b, l, ATTN_W), BF16)] + out_shape
                     + [jax.ShapeDtypeStruct((b, l, BRANCH_W), BF16),
                        jax.ShapeDtypeStruct((b, l, 2 * BRANCH_W), BF16),
                        jax.ShapeDtypeStruct((b, l, N_BRANCH * d), BF16)])
    return pl.pallas_call(
        functools.partial(_inproj_kernel, d=d, kv_only=kv_only),
        grid=(b, l // tm),
        in_specs=[
            tok(d),
            _mod_spec(d, mod_row),
            _const_spec((d, in_w)),
            tab, tab, tab,
            _const_spec((1, ATTN_W)), _const_spec((1, KV_W)), _const_spec((ATTN_W, ATTN_W)),
            _const_spec((N_GROUPS, CHUNK, CHUNK)), _const_spec((N_GROUPS, CHUNK, GROUP_W)),
            _const_spec((GROUP_W, 2 * GROUP_W)),
        ],
        out_specs=out_specs,
        out_shape=out_shape,
        compiler_params=_cparams(("parallel", "parallel")),
        name="in_projection_kv" if kv_only else "in_projection",
    )(x, mod, w_in, cos, sin_a, sin_b, qg, kg, gmat, sgw, sgb, dftc)


def _attn_kernel(q_ref, kt_ref, v_ref, o_ref):
    tq = q_ref.shape[0]
    for kv in range(N_KV_HEADS):
        kt = kt_ref[kv * HEAD_DIM:(kv + 1) * HEAD_DIM, :]
        v = v_ref[:, kv * HEAD_DIM:(kv + 1) * HEAD_DIM]
        heads = [kv * GQA_GROUP + g for g in range(GQA_GROUP)]
        qs = jnp.concatenate([q_ref[:, hd * HEAD_DIM:(hd + 1) * HEAD_DIM] for hd in heads], axis=0)
        s = _dot(qs, kt)
        p = jnp.exp(s - jnp.max(s, axis=-1, keepdims=True))
        l = jnp.sum(p, axis=-1, keepdims=True)
        o = _dot(p.astype(BF16), v) / l
        for g, hd in enumerate(heads):
            o_ref[:, hd * HEAD_DIM:(hd + 1) * HEAD_DIM] = o[g * tq:(g + 1) * tq, :].astype(BF16)


def _attention(q, kt, v, *, tq):
    b, l, _ = q.shape
    t = kt.shape[2]
    return pl.pallas_call(
        _attn_kernel,
        grid=(b, l // tq),
        in_specs=[
            pl.BlockSpec((None, tq, ATTN_W), lambda bi, i: (bi, i, 0)),
            pl.BlockSpec((None, KV_W, t), lambda bi, i: (bi, 0, 0)),
            pl.BlockSpec((None, t, KV_W), lambda bi, i: (bi, 0, 0)),
        ],
        out_specs=pl.BlockSpec((None, tq, ATTN_W), lambda bi, i: (bi, i, 0)),
        out_shape=jax.ShapeDtypeStruct((b, l, ATTN_W), BF16),
        compiler_params=_cparams(("parallel", "parallel")),
        name="attention",
    )(q, kt, v)


def _dft_cos_sin(n):
    k = np.arange(n, dtype=np.float64)
    ang = 2.0 * np.pi * np.outer(k, k) / n
    return np.cos(ang), np.sin(ang)


def _fft_stage1_kernel(x_ref, w_ref, tc_ref, ts_ref, o_ref, *, tb):
    l1 = x_ref.shape[0]
    for j in range(tb):
        x = x_ref[:, j * 2 * BRANCH_W:(j + 1) * 2 * BRANCH_W]
        stacked = jnp.concatenate([x[:, :BRANCH_W], x[:, BRANCH_W:]], axis=0)
        u = _dot(w_ref[...], stacked)
        tc = jnp.concatenate([tc_ref[j]] * N_GROUPS, axis=1)
        ts = jnp.concatenate([ts_ref[j]] * N_GROUPS, axis=1)
        ur, ui = u[:l1], u[l1:]
        o_ref[0, j] = (ur * tc + ui * ts).astype(BF16)
        o_ref[1, j] = (ui * tc - ur * ts).astype(BF16)


def _fft_stage2_kernel(z_ref, w_ref, o_ref, *, scale):
    o_ref[...] = (_dot(w_ref[...], z_ref[...]) * scale).astype(o_ref.dtype)


def _fourier_latent(ftcs, *, tb, tl):
    b, s, _ = ftcs.shape
    l2 = FFT_L2
    l1 = s // l2
    c1, s1 = _dft_cos_sin(l1)
    c2, s2 = _dft_cos_sin(l2)
    w1 = jnp.asarray(np.block([[c1, -s1], [-s1, -c1]]), BF16)
    w2 = jnp.asarray(np.concatenate([c2, s2], axis=1), BF16)
    ang = 2.0 * np.pi * np.outer(np.arange(l2), np.arange(l1)) / s
    tc = jnp.asarray(np.repeat(np.cos(ang)[:, :, None], GROUP_W, axis=2), F32)
    ts = jnp.asarray(np.repeat(np.sin(ang)[:, :, None], GROUP_W, axis=2), F32)
    x3 = ftcs.reshape(b, l1, l2 * 2 * BRANCH_W)
    z = pl.pallas_call(
        functools.partial(_fft_stage1_kernel, tb=tb),
        grid=(b, l2 // tb),
        in_specs=[
            pl.BlockSpec((None, l1, tb * 2 * BRANCH_W), lambda bi, i: (bi, 0, i)),
            _const_spec((2 * l1, 2 * l1)),
            pl.BlockSpec((tb, l1, GROUP_W), lambda bi, i: (i, 0, 0)),
            pl.BlockSpec((tb, l1, GROUP_W), lambda bi, i: (i, 0, 0)),
        ],
        out_specs=pl.BlockSpec((None, 2, tb, l1, BRANCH_W), lambda bi, i: (bi, 0, i, 0, 0)),
        out_shape=jax.ShapeDtypeStruct((b, 2, l2, l1, BRANCH_W), BF16),
        compiler_params=_cparams(("parallel", "parallel")),
        name="fourier_stage1",
    )(x3, w1, tc, ts)
    zz = z.reshape(b, 2 * l2, l1 * BRANCH_W)
    y = pl.pallas_call(
        functools.partial(_fft_stage2_kernel, scale=1.0 / math.sqrt(s * GROUP_W)),
        grid=(b, l1 * BRANCH_W // tl),
        in_specs=[
            pl.BlockSpec((None, 2 * l2, tl), lambda bi, i: (bi, 0, i)),
            _const_spec((l2, 2 * l2)),
        ],
        out_specs=pl.BlockSpec((None, l2, tl), lambda bi, i: (bi, 0, i)),
        out_shape=jax.ShapeDtypeStruct((b, l2, l1 * BRANCH_W), BF16),
        compiler_params=_cparams(("parallel", "parallel")),
        name="fourier_stage2",
    )(zz, w2)
    return y.reshape(b, s, BRANCH_W)


def _fft_direct_kernel(x_ref, w_ref, o_ref, *, scale):
    x = x_ref[...]
    stacked = jnp.concatenate([x[:, :BRANCH_W], x[:, BRANCH_W:]], axis=0)
    o_ref[...] = (_dot(w_ref[...], stacked) * scale).astype(o_ref.dtype)


def _fourier_direct(ftcs):
    b, l, _ = ftcs.shape
    c, s = _dft_cos_sin(l)
    w = jnp.asarray(np.concatenate([c, -s], axis=1), BF16)
    return pl.pallas_call(
        functools.partial(_fft_direct_kernel, scale=1.0 / math.sqrt(l * GROUP_W)),
        grid=(b,),
        in_specs=[pl.BlockSpec((None, l, 2 * BRANCH_W), lambda bi: (bi, 0, 0)), _const_spec((l, 2 * l))],
        out_specs=pl.BlockSpec((None, l, BRANCH_W), lambda bi: (bi, 0, 0)),
        out_shape=jax.ShapeDtypeStruct((b, l, BRANCH_W), BF16),
        compiler_params=_cparams(("parallel",)),
        name="fourier_direct",
    )(ftcs, w)


def _post_norm(x, y, gain, bias, alpha):
    return _normalize(alpha * x + y) * gain + bias


def _merge_kernel(ya_ref, ysg_ref, yft_ref, gate_ref, x_ref, m_ref, wb_ref, wo_ref, g_ref, b_ref, o_ref, *, d, alpha):
    acc = None
    for n, y_ref in enumerate((ya_ref, ysg_ref, yft_ref)):
        p = gate_ref[:, n * d:(n + 1) * d].astype(F32) * _dot(y_ref[...], wb_ref[n])
        acc = p if acc is None else acc + p
    y = _dot(acc.astype(BF16), wo_ref[...])
    o_ref[...] = _post_norm(x_ref[...], m_ref[:, 2 * d:3 * d] * y, g_ref[...], b_ref[...], alpha)


def _merge(ya, ysg, yft, gate, x, mod, mod_row, wb, wo, gain, bias, *, tm, alpha):
    b, l, d = x.shape
    tok = lambda w: pl.BlockSpec((None, tm, w), lambda bi, i: (bi, i, 0))
    return pl.pallas_call(
        functools.partial(_merge_kernel, d=d, alpha=alpha),
        grid=(b, l // tm),
        in_specs=[tok(BRANCH_W), tok(BRANCH_W), tok(BRANCH_W), tok(N_BRANCH * d), tok(d),
                  _mod_spec(d, mod_row),
                  _const_spec((N_BRANCH, BRANCH_W, d)), _const_spec((d, d)),
                  _const_spec((1, d)), _const_spec((1, d))],
        out_specs=tok(d),
        out_shape=jax.ShapeDtypeStruct((b, l, d), F32),
        compiler_params=_cparams(("parallel", "parallel")),
        name="merge",
    )(ya, ysg, yft, gate, x, mod, wb, wo, gain, bias)


def _ffn_kernel(x_ref, m_ref, wg_ref, wu_ref, wd_ref, g_ref, b_ref, o_ref, *, d, alpha):
    x = x_ref[...]
    h = (_normalize(x) * (1.0 + m_ref[:, 4 * d:5 * d]) + m_ref[:, 3 * d:4 * d]).astype(BF16)
    gate = _dot(h, wg_ref[...])
    act = (gate * _sigmoid(gate) * _dot(h, wu_ref[...])).astype(BF16)
    f = _dot(act, wd_ref[...])
    o_ref[...] = _post_norm(x, m_ref[:, 5 * d:6 * d] * f, g_ref[...], b_ref[...], alpha)


def _dense_ffn(x, mod, mod_row, wg, wu, wd, gain, bias, *, tm, alpha):
    b, l, d = x.shape
    dff = wg.shape[1]
    tok = pl.BlockSpec((None, tm, d), lambda bi, i: (bi, i, 0))
    return pl.pallas_call(
        functools.partial(_ffn_kernel, d=d, alpha=alpha),
        grid=(b, l // tm),
        in_specs=[tok, _mod_spec(d, mod_row),
                  _const_spec((d, dff)), _const_spec((d, dff)), _const_spec((dff, d)),
                  _const_spec((1, d)), _const_spec((1, d))],
        out_specs=tok,
        out_shape=jax.ShapeDtypeStruct((b, l, d), F32),
        compiler_params=_cparams(("parallel", "parallel")),
        name="dense_ffn",
    )(x, mod, wg, wu, wd, gain, bias)


def _router_kernel(x_ref, m_ref, rhi_ref, rlo_ref, tri_ref, hp_ref, idx_ref, wts_ref, rank_ref, cnt_ref, carry,
                   *, d, n_exp):
    @pl.when((pl.program_id(0) == 0) & (pl.program_id(1) == 0))
    def _():
        carry[...] = jnp.zeros_like(carry)

    tm = x_ref.shape[0]
    h = _normalize(x_ref[...]) * (1.0 + m_ref[:, 4 * d:5 * d]) + m_ref[:, 3 * d:4 * d]
    hp_ref[...] = _pack_rows(h)
    hi, lo = _split_bf16(h)
    logits = _dot_nt(rhi_ref[...], hi) + _dot_nt(rhi_ref[...], lo) + _dot_nt(rlo_ref[...], hi)
    eid = lax.broadcasted_iota(I32, (n_exp, tm), 0).astype(F32)
    v1 = jnp.max(logits, axis=0, keepdims=True)
    i1 = jnp.min(jnp.where(logits == v1, eid, float(n_exp)), axis=0, keepdims=True)
    sel1 = eid == i1
    rest = jnp.where(sel1, -jnp.inf, logits)
    v2 = jnp.max(rest, axis=0, keepdims=True)
    i2 = jnp.min(jnp.where(rest == v2, eid, float(n_exp)), axis=0, keepdims=True)
    sel2 = eid == i2
    w1 = 1.0 / (1.0 + jnp.exp(v2 - v1))
    onehot = jnp.where(sel1 | sel2, 1.0, 0.0)
    before = _dot(onehot.astype(BF16), tri_ref[...]) + carry[:, 0:1]
    idx_ref[0:1, :] = i1.astype(I32)
    idx_ref[1:2, :] = i2.astype(I32)
    wts_ref[...] = jnp.zeros_like(wts_ref)
    wts_ref[0:1, :] = w1
    wts_ref[1:2, :] = 1.0 - w1
    rank_ref[0:1, :] = jnp.sum(jnp.where(sel1, before, 0.0), axis=0, keepdims=True).astype(I32)
    rank_ref[1:2, :] = jnp.sum(jnp.where(sel2, before, 0.0), axis=0, keepdims=True).astype(I32)
    carry[...] = carry[...] + jnp.sum(onehot, axis=1, keepdims=True)
    cnt_ref[...] = carry[...]


def _route(x, mod, mod_row, router, *, tm):
    b, l, d = x.shape
    n_exp = router.shape[1]
    n = b * l
    rt = router.T
    rhi = rt.astype(BF16)
    rlo = (rt - rhi.astype(F32)).astype(BF16)
    tri = jnp.asarray(np.triu(np.ones((tm, tm)), 1), BF16)
    per_b = l // tm
    flat = lambda bi, i: (0, bi * per_b + i)
    return pl.pallas_call(
        functools.partial(_router_kernel, d=d, n_exp=n_exp),
        grid=(b, per_b),
        in_specs=[pl.BlockSpec((None, tm, d), lambda bi, i: (bi, i, 0)),
                  _mod_spec(d, mod_row),
                  _const_spec((n_exp, d)), _const_spec((n_exp, d)), _const_spec((tm, tm))],
        out_specs=[pl.BlockSpec((tm, d // 2), lambda bi, i: (bi * per_b + i, 0)),
                   pl.BlockSpec((2, tm), flat), pl.BlockSpec((8, tm), flat), pl.BlockSpec((2, tm), flat),
                   pl.BlockSpec((n_exp, LANES), lambda bi, i: (0, 0))],
        out_shape=[jax.ShapeDtypeStruct((n, d // 2), U32),
                   jax.ShapeDtypeStruct((2, n), I32), jax.ShapeDtypeStruct((8, n), F32),
                   jax.ShapeDtypeStruct((2, n), I32), jax.ShapeDtypeStruct((n_exp, LANES), F32)],
        scratch_shapes=[pltpu.VMEM((n_exp, LANES), F32)],
        compiler_params=_cparams(("arbitrary", "arbitrary")),
        name="router",
    )(x, mod, rhi, rlo, tri)


def _row_copy(src_ref, src_row, dst_ref, dst_row, sem):
    return pltpu.make_async_copy(src_ref.at[pl.ds(src_row, 1)], dst_ref.at[pl.ds(dst_row, 1)], sem)


def _scatter_kernel(pos_ref, h_ref, init_ref, xs_ref, sem, *, tm):
    del init_ref
    base = pl.program_id(0) * tm

    def issue(t, carry):
        for k in range(2):
            _row_copy(h_ref, base + t, xs_ref, pos_ref[k, t], sem).start()
        return carry

    lax.fori_loop(0, tm, issue, 0)
    pltpu.make_async_copy(h_ref.at[pl.ds(0, 2 * tm)], xs_ref.at[pl.ds(0, 2 * tm)], sem).wait()


def _scatter_rows(pos, hp, n_rows, *, tm):
    n, w = hp.shape
    return pl.pallas_call(
        functools.partial(_scatter_kernel, tm=tm),
        grid=(n // tm,),
        in_specs=[pl.BlockSpec((2, tm), lambda i: (0, i), memory_space=pltpu.SMEM),
                  pl.BlockSpec(memory_space=pl.ANY), pl.BlockSpec(memory_space=pl.ANY)],
        out_specs=pl.BlockSpec(memory_space=pl.ANY),
        out_shape=jax.ShapeDtypeStruct((n_rows, w), U32),
        scratch_shapes=[pltpu.SemaphoreType.DMA(())],
        input_output_aliases={2: 0},
        compiler_params=_cparams(("arbitrary",)),
        name="scatter_rows",
    )(pos, hp, jnp.zeros((n_rows, w), U32))


def _expert_kernel(te_ref, nu_ref, x_ref, wg_ref, wu_ref, wd_ref, o_ref, h_scr, acc):
    del te_ref
    i, j = pl.program_id(0), pl.program_id(1)

    @pl.when((i >= nu_ref[0]) & (j == 0))
    def _():
        o_ref[...] = jnp.zeros_like(o_ref)

    @pl.when(i < nu_ref[0])
    def _():
        @pl.when(j == 0)
        def _():
            h_scr[...] = _unpack_rows(x_ref[...]).astype(BF16)
            acc[...] = jnp.zeros_like(acc)

        h = h_scr[...]
        gate = _dot(h, wg_ref[...])
        act = (gate * _sigmoid(gate) * _dot(h, wu_ref[...])).astype(BF16)
        acc[...] += _dot(act, wd_ref[...])

        @pl.when(j == pl.num_programs(1) - 1)
        def _():
            o_ref[...] = _pack_rows(acc[...])


def _expert_ffn(tile_expert, n_used, xs, wg, wu, wd, *, tm, tf):
    n_rows, w = xs.shape
    n_exp, d, dff = wg.shape
    tf = tf if dff % tf == 0 else dff
    live =lambda i, nu: jnp.minimum(i, nu[0] - 1)
    grid_spec = pltpu.PrefetchScalarGridSpec(
        num_scalar_prefetch=2,
        grid=(n_rows // tm, dff // tf),
        in_specs=[pl.BlockSpec((tm, w), lambda i, j, te, nu: (live(i, nu), 0)),
                  pl.BlockSpec((None, d, tf), lambda i, j, te, nu: (te[live(i, nu)], 0, jnp.where(i < nu[0], j, dff // tf - 1))),
                  pl.BlockSpec((None, d, tf), lambda i, j, te, nu: (te[live(i, nu)], 0, jnp.where(i < nu[0], j, dff // tf - 1))),
                  pl.BlockSpec((None, tf, d), lambda i, j, te, nu: (te[live(i, nu)], jnp.where(i < nu[0], j, dff // tf - 1), 0))],
        out_specs=pl.BlockSpec((tm, w), lambda i, j, te, nu: (i, 0)),
        scratch_shapes=[pltpu.VMEM((tm, d), BF16), pltpu.VMEM((tm, d), F32)],
    )
    return pl.pallas_call(
        _expert_kernel,
        grid_spec=grid_spec,
        out_shape=jax.ShapeDtypeStruct((n_rows, w), U32),
        compiler_params=_cparams(("arbitrary", "arbitrary")),
        name="expert_ffn",
    )(tile_expert, n_used, xs, wg, wu, wd)


def _combine_kernel(pos_ref, ys_ref, wts_ref, x_ref, m_ref, g_ref, b_ref, o_ref, buf, sem, *, d, alpha):
    tm = x_ref.shape[0]

    def issue(t, carry):
        for k in range(2):
            pltpu.make_async_copy(ys_ref.at[pl.ds(pos_ref[k, t], 1)], buf.at[k, pl.ds(t, 1)], sem).start()
        return carry

    lax.fori_loop(0, tm, issue, 0)
    for k in range(2):
        pltpu.make_async_copy(ys_ref.at[pl.ds(0, tm)], buf.at[k], sem).wait()
    w = wts_ref[...].T
    f = w[:, 0:1] * _unpack_rows(buf[0]) + w[:, 1:2] * _unpack_rows(buf[1])
    o_ref[...] = _post_norm(x_ref[...], m_ref[:, 5 * d:6 * d] * f, g_ref[...], b_ref[...], alpha)


def _combine(pos, ys, wts, x, mod, mod_row, gain, bias, *, tm, alpha):
    b, l, d = x.shape
    per_b = l // tm
    flat = lambda bi, i: (0, bi * per_b + i)
    tok = pl.BlockSpec((None, tm, d), lambda bi, i: (bi, i, 0))
    return pl.pallas_call(
        functools.partial(_combine_kernel, d=d, alpha=alpha),
        grid=(b, per_b),
        in_specs=[pl.BlockSpec((2, tm), flat, memory_space=pltpu.SMEM),
                  pl.BlockSpec(memory_space=pl.ANY),
                  pl.BlockSpec((8, tm), flat), tok,
                  _mod_spec(d, mod_row),
                  _const_spec((1, d)), _const_spec((1, d))],
        out_specs=tok,
        out_shape=jax.ShapeDtypeStruct((b, l, d), F32),
        scratch_shapes=[pltpu.VMEM((2, tm, d // 2), U32), pltpu.SemaphoreType.DMA(())],
        compiler_params=_cparams(("arbitrary", "arbitrary")),
        name="combine",
    )(pos, ys, wts, x, mod, gain, bias)


def _moe_ffn(x, mod, mod_row, router, wg, wu, wd, gain, bias, *, tm, tm_e, tf, alpha):
    b, l, d = x.shape
    n = b * l
    n_exp = router.shape[1]
    hp, idx, wts, rank, counts = _route(x, mod, mod_row, router, tm=tm)
    cnt = counts[:, 0].astype(I32)
    tiles = (cnt + tm_e - 1) // tm_e
    tile_end = jnp.cumsum(tiles)
    offsets = (tile_end - tiles) * tm_e
    pos = offsets[idx] + rank
    n_tiles = (2 * n) // tm_e + n_exp
    tile_expert = jnp.minimum(jnp.searchsorted(tile_end, jnp.arange(n_tiles, dtype=I32), side="right"),
                              n_exp - 1).astype(I32)
    n_used = tile_end[-1:].astype(I32)
    xs = _scatter_rows(pos, hp, n_tiles * tm_e, tm=tm)
    ys = _expert_ffn(tile_expert, n_used, xs, wg, wu, wd, tm=tm_e, tf=tf)
    return _combine(pos, ys, wts, x, mod, mod_row, gain, bias, tm=tm, alpha=alpha)


def _token_mixer_consts(q_norm, k_norm, sg_w, sg_b):
    qg = (jnp.tile(q_norm, N_Q_HEADS) * (HEAD_DIM ** -0.5)).reshape(1, ATTN_W)
    kg = jnp.tile(k_norm, N_KV_HEADS).reshape(1, KV_W)
    sgb = jnp.broadcast_to(sg_b[:, :, None], (N_GROUPS, CHUNK, GROUP_W))
    return qg, kg, sg_w.astype(BF16), sgb


def kernel(x, c, ctx, c_ctx, w_mod, b_mod, w_in, q_norm, k_norm, sg_w, sg_b, w_branch, w_out, ln1_g, ln1_b, ln2_g, ln2_b, ffn_w_gate, ffn_w_up, ffn_w_down, router, exp_w_gate, exp_w_up, exp_w_down):
    b, s, d = x.shape
    cl = ctx.shape[1]
    depth = w_in.shape[0]
    assert s % (GRID_W * FFT_L2) == 0 and s % 512 == 0 and cl % CHUNK == 0 and b + 1 <= MOD_ROWS
    alpha = (2 * depth) ** 0.25

    cond = jnp.zeros((MOD_ROWS, d), F32).at[:b].set(c).at[b].set(c_ctx)
    mod = _modulation(cond, w_mod, b_mod).reshape(depth, MOD_ROWS, 1, 6 * d)

    t = np.arange(s)
    lat_tables = _rope_tables(t // GRID_W, t % GRID_W)
    ctx_tables = _rope_tables(np.zeros(cl), np.zeros(cl))
    head = np.arange(ATTN_W) // HEAD_DIM
    gmat = jnp.asarray(head[:, None] == head[None, :], BF16)
    cc, sc = _dft_cos_sin(GROUP_W)
    dftc = jnp.asarray(np.concatenate([cc, sc], axis=1), BF16)
    row2 = lambda v: v.reshape(1, d)

    x_lat, x_ctx = x, ctx
    for l in range(depth):
        last = l == depth - 1
        m = mod[l]
        consts = _token_mixer_consts(q_norm[l], k_norm[l], sg_w[l], sg_b[l])
        w_in_l = w_in[l].astype(BF16)
        wb, wo = w_branch[l].astype(BF16), w_out[l].astype(BF16)
        proj = functools.partial(_in_projection, w_in=w_in_l, qg=consts[0], kg=consts[1], gmat=gmat,
                                 sgw=consts[2], sgb=consts[3], dftc=dftc)
        q, kt, v, ysg, ftcs, gate = proj(x_lat, m, None, tables=lat_tables, tm=512, kv_only=False)
        ctx_out = proj(x_ctx, m, b, tables=ctx_tables, tm=cl, kv_only=last)
        kt_c, v_c = ctx_out[:2] if last else ctx_out[1:3]
        ya = _attention(q, jnp.concatenate([kt_c, kt], axis=2), jnp.concatenate([v_c, v], axis=1), tq=256)
        yft = _fourier_latent(ftcs, tb=8, tl=4096)
        x_lat = _merge(ya, ysg, yft, gate, x_lat, m, None, wb, wo, row2(ln1_g[l]), row2(ln1_b[l]), tm=512, alpha=alpha)
        if not last:
            q_c, _, _, ysg_c, ftcs_c, gate_c = ctx_out
            ya_c = _attention(q_c, kt_c, v_c, tq=cl)
            x_ctx = _merge(ya_c, ysg_c, _fourier_direct(ftcs_c), gate_c, x_ctx, m, b, wb, wo,
                           row2(ln1_g[l]), row2(ln1_b[l]), tm=cl, alpha=alpha)
        i = l // 2
        if l % 2 == 0:
            wg, wu, wd = ffn_w_gate[i].astype(BF16), ffn_w_up[i].astype(BF16), ffn_w_down[i].astype(BF16)
            ffn = functools.partial(_dense_ffn, wg=wg, wu=wu, wd=wd, gain=row2(ln2_g[l]), bias=row2(ln2_b[l]), alpha=alpha)
            x_lat = ffn(x_lat, m, None, tm=512)
            if not last:
                x_ctx = ffn(x_ctx, m, b, tm=cl)
        else:
            moe = functools.partial(_moe_ffn, router=router[i], wg=exp_w_gate[i].astype(BF16),
                                    wu=exp_w_up[i].astype(BF16), wd=exp_w_down[i].astype(BF16),
                                    gain=row2(ln2_g[l]), bias=row2(ln2_b[l]), tm_e=512, tf=1792, alpha=alpha)
            x_lat = moe(x_lat, m, None, tm=512)
            if not last:
                x_ctx = moe(x_ctx, m, b, tm=cl)
    return x_lat
```

```python
import functools
import math

import numpy as np
import jax
import jax.numpy as jnp
from jax import lax
from jax.experimental import pallas as pl
from jax.experimental.pallas import tpu as pltpu

F32 = jnp.float32
BF16 = jnp.bfloat16
I32 = jnp.int32

N_Q_HEADS = 8
N_KV_HEADS = 2
GQA_GROUP = N_Q_HEADS // N_KV_HEADS
HEAD_DIM = 64
ATTN_W = N_Q_HEADS * HEAD_DIM
KV_W = N_KV_HEADS * HEAD_DIM
GRID_W = 64
ROPE_THETA = 10000.0
CHUNK = 128
N_GROUPS = 4
GROUP_W = 128
BRANCH_W = N_GROUPS * GROUP_W
N_BRANCH = 3
OFF_Q = 0
OFF_K = OFF_Q + ATTN_W
OFF_V = OFF_K + KV_W
OFF_U = OFF_V + KV_W
OFF_SGV = OFF_U + BRANCH_W
OFF_FT = OFF_SGV + BRANCH_W
OFF_GATE = OFF_FT + BRANCH_W
LN_EPS = 1e-6
RMS_EPS = 1e-6
GELU_C = math.sqrt(2.0 / math.pi)

LANES = 128
V7X_VMEM_BYTES = 64 * 1024 * 1024
VMEM_LIMIT = V7X_VMEM_BYTES - 8 * 1024 * 1024
SUBLANES = 8
FFT_L2 = SUBLANES * SUBLANES
MOD_ROWS = 16


def _cparams(sem):
    return pltpu.CompilerParams(dimension_semantics=sem, vmem_limit_bytes=VMEM_LIMIT)


def _const_spec(shape):
    nd = len(shape)
    return pl.BlockSpec(shape, lambda *_: (0,) * nd, pipeline_mode=pl.Buffered(1))


def _mod_spec(d, mod_row):
    if mod_row is None:
        return pl.BlockSpec((None, 1, 6 * d), lambda bi, i: (bi, 0, 0))
    return pl.BlockSpec((None, 1, 6 * d), lambda bi, i: (mod_row, 0, 0))


def _bf16_const(a):
    return jnp.asarray(a, F32).astype(BF16)


def _dot(a, b):
    return jnp.dot(a, b, preferred_element_type=F32)


def _dot_nt(a, b):
    return lax.dot_general(a, b, (((1,), (1,)), ((), ())), preferred_element_type=F32)


def _sigmoid(x):
    return 0.5 * jnp.tanh(0.5 * x) + 0.5


def _gelu_tanh(x):
    return 0.5 * x * (1.0 + jnp.tanh(GELU_C * (x + 0.044715 * (x * x * x))))


def _normalize(x):
    mu = jnp.mean(x, axis=-1, keepdims=True)
    xc = x - mu
    var = jnp.mean(xc * xc, axis=-1, keepdims=True)
    return xc * lax.rsqrt(var + LN_EPS)


def _split_bf16(x):
    hi = x.astype(BF16)
    lo = (x - hi.astype(F32)).astype(BF16)
    return hi, lo


def _mod_kernel(c_ref, w_ref, b_ref, o_ref):
    c = c_ref[...]
    s = c * _sigmoid(c)
    o_ref[...] = _dot(s.astype(BF16), w_ref[...].astype(BF16)) + b_ref[...]


def _modulation(cond, w_mod, b_mod):
    depth, d, d6 = w_mod.shape
    tn = 2 * d if d6 % (2 * d) == 0 else d
    return pl.pallas_call(
        _mod_kernel,
        grid=(depth, d6 // tn),
        in_specs=[
            pl.BlockSpec((MOD_ROWS, d), lambda l, j: (0, 0)),
            pl.BlockSpec((None, d, tn), lambda l, j: (l, 0, j)),
            pl.BlockSpec((None, 1, tn), lambda l, j: (l, 0, j)),
        ],
        out_specs=pl.BlockSpec((None, MOD_ROWS, tn), lambda l, j: (l, 0, j)),
        out_shape=jax.ShapeDtypeStruct((depth, MOD_ROWS, d6), F32),
        compiler_params=_cparams(("parallel", "parallel")),
        name="modulation",
    )(cond, w_mod, b_mod.reshape(depth, 1, d6))


def _rope_tables(pos_row, pos_col):
    half = HEAD_DIM // 2
    d2 = half // 2
    lane = np.arange(LANES) % HEAD_DIM
    inv = ROPE_THETA ** (-(lane % d2).astype(np.float64) / d2)
    pos = np.where((lane < half)[None, :], pos_row[:, None], pos_col[:, None]).astype(np.float64)
    ang = pos * inv[None, :]
    first = ((lane % half) < d2)[None, :]
    cos = np.cos(ang)
    sin = np.sin(ang)
    sin_a = np.where(first, -sin, 0.0)
    sin_b = np.where(first, 0.0, sin)
    return (jnp.asarray(cos, F32), jnp.asarray(sin_a, F32), jnp.asarray(sin_b, F32))


def _rope(x, cos, sin_a, sin_b):
    d2 = HEAD_DIM // 4
    return x * cos + pltpu.roll(x, LANES - d2, 1) * sin_a + pltpu.roll(x, d2, 1) * sin_b


def _head_rms(z, gmat, gain):
    hi, lo = _split_bf16(z * z)
    ss = _dot(hi, gmat) + _dot(lo, gmat)
    return z * lax.rsqrt(ss * (1.0 / HEAD_DIM) + RMS_EPS) * gain


def _inproj_kernel(x_ref, m_ref, w_ref, cos_ref, sa_ref, sb_ref, qg_ref, kg_ref, gm_ref, sgw_ref, sgb_ref,
                   dft_ref, *out_refs, d, kv_only):
    tm = x_ref.shape[0]
    h = (_normalize(x_ref[...]) * (1.0 + m_ref[:, d:2 * d]) + m_ref[:, 0:d]).astype(BF16)
    cos, sin_a, sin_b = cos_ref[...], sa_ref[...], sb_ref[...]

    def proj(a, b):
        return _dot(h, w_ref[:, a:b])

    if kv_only:
        k_ref, vt_ref = out_refs
    else:
        q_ref, k_ref, vt_ref, ysg_ref, ft_ref, gate_ref = out_refs

    def finish_q(z):
        qn = _head_rms(z, gm_ref[...], qg_ref[...])
        for s in range(ATTN_W // LANES):
            sl = slice(s * LANES, (s + 1) * LANES)
            q_ref[:, sl] = _rope(qn[:, sl], cos, sin_a, sin_b).astype(BF16)

    def finish_kv(z):
        kn = _head_rms(z[:, :KV_W], gm_ref[0:KV_W, 0:KV_W], kg_ref[...])
        k_ref[...] = _rope(kn, cos, sin_a, sin_b).astype(BF16)
        vt_ref[...] = z[:, KV_W:].T.astype(BF16)

    def finish_gating(z):
        u = _gelu_tanh(z[:, :BRANCH_W])
        v2 = _gelu_tanh(z[:, BRANCH_W:])
        for g in range(N_GROUPS):
            gl = slice(g * GROUP_W, (g + 1) * GROUP_W)
            vg = _normalize(v2[:, gl]).astype(BF16)
            for c in range(tm // CHUNK):
                rows = slice(c * CHUNK, (c + 1) * CHUNK)
                mixed = _dot(sgw_ref[g], vg[rows, :]) + sgb_ref[g]
                ysg_ref[rows, gl] = (u[rows, gl] * mixed).astype(BF16)

    def finish_fourier(z):
        zf = z.astype(BF16)
        for g in range(N_GROUPS):
            cs = _dot(zf[:, g * GROUP_W:(g + 1) * GROUP_W], dft_ref[...])
            ft_ref[:, g * GROUP_W:(g + 1) * GROUP_W] = cs[:, :GROUP_W]
            ft_ref[:, BRANCH_W + g * GROUP_W:BRANCH_W + (g + 1) * GROUP_W] = cs[:, GROUP_W:]

    def finish_gate(n):
        def store(z):
            gate_ref[:, n * d:(n + 1) * d] = _sigmoid(z).astype(BF16)
        return store

    if kv_only:
        finish_kv(proj(OFF_K, OFF_U))
        return
    stages = [((OFF_Q, OFF_K), finish_q), ((OFF_K, OFF_U), finish_kv), ((OFF_U, OFF_FT), finish_gating),
              ((OFF_FT, OFF_GATE), finish_fourier)]
    stages += [((OFF_GATE + n * d, OFF_GATE + (n + 1) * d), finish_gate(n)) for n in range(N_BRANCH)]
    z = proj(*stages[0][0])
    for i, (_, finish) in enumerate(stages):
        z_next = proj(*stages[i + 1][0]) if i + 1 < len(stages) else None
        finish(z)
        z = z_next


def _in_projection(x, mod, mod_row, w_in, tables, qg, kg, gmat, sgw, sgb, dftc, *, tm, kv_only):
    b, l, d = x.shape
    in_w = w_in.shape[1]
    cos, sin_a, sin_b = tables
    tok = lambda w: pl.BlockSpec((None, tm, w), lambda bi, i: (bi, i, 0))
    tab = pl.BlockSpec((tm, LANES), lambda bi, i: (i, 0))
    out_specs = [tok(KV_W), pl.BlockSpec((None, KV_W, tm), lambda bi, i: (bi, 0, i))]
    out_shape = [jax.ShapeDtypeStruct((b, l, KV_W), BF16), jax.ShapeDtypeStruct((b, KV_W, l), BF16)]
    if not kv_only:
        out_specs = [tok(ATTN_W)] + out_specs + [tok(BRANCH_W), tok(2 * BRANCH_W), tok(N_BRANCH * d)]
        out_shape = ([jax.ShapeDtypeStruct((b, l, ATTN_W), BF16)] + out_shape
                     + [jax.ShapeDtypeStruct((b, l, BRANCH_W), BF16),
                        jax.ShapeDtypeStruct((b, l, 2 * BRANCH_W), F32),
                        jax.ShapeDtypeStruct((b, l, N_BRANCH * d), BF16)])
    return pl.pallas_call(
        functools.partial(_inproj_kernel, d=d, kv_only=kv_only),
        grid=(b, l // tm),
        in_specs=[
            tok(d),
            _mod_spec(d, mod_row),
            _const_spec((d, in_w)),
            tab, tab, tab,
            _const_spec((1, ATTN_W)), _const_spec((1, KV_W)), _const_spec((ATTN_W, ATTN_W)),
            _const_spec((N_GROUPS, CHUNK, CHUNK)), _const_spec((N_GROUPS, CHUNK, GROUP_W)),
            _const_spec((GROUP_W, 2 * GROUP_W)),
        ],
        out_specs=out_specs,
        out_shape=out_shape,
        compiler_params=_cparams(("parallel", "parallel")),
        name="in_projection_kv" if kv_only else "in_projection",
    )(x, mod, w_in, cos, sin_a, sin_b, qg, kg, gmat, sgw, sgb, dftc)


def _attn_kernel(q_ref, k_ref, vt_ref, o_ref, p_scr):
    t = k_ref.shape[0]
    rb = 256 if t % 256 == 0 else LANES
    nb = t // rb

    def scores(hd):
        kv = hd // GQA_GROUP
        qt = q_ref[:, hd * HEAD_DIM:(hd + 1) * HEAD_DIM].astype(F32).T.astype(BF16)
        return _dot(k_ref[:, kv * HEAD_DIM:(kv + 1) * HEAD_DIM], qt)

    def softmax(s, slot):
        macc = s[0:rb, :]
        for j in range(1, nb):
            macc = jnp.maximum(macc, s[j * rb:(j + 1) * rb, :])
        m = jnp.max(macc, axis=0, keepdims=True)
        lacc = None
        for j in range(nb):
            p = jnp.exp2(s[j * rb:(j + 1) * rb, :] - m)
            lacc = p if lacc is None else lacc + p
            p_scr[slot, j * rb:(j + 1) * rb, :] = p.astype(BF16)
        return jnp.sum(lacc, axis=0, keepdims=True)

    def weighted_values(hd, slot, l):
        kv = hd // GQA_GROUP
        o = _dot(vt_ref[kv * HEAD_DIM:(kv + 1) * HEAD_DIM, :], p_scr[slot]) * (1.0 / l)
        o_ref[:, hd * HEAD_DIM:(hd + 1) * HEAD_DIM] = o.T.astype(BF16)

    s_next = scores(0)
    l_prev = None
    for hd in range(N_Q_HEADS):
        s_cur = s_next
        if hd + 1 < N_Q_HEADS:
            s_next = scores(hd + 1)
        if hd >= 1:
            weighted_values(hd - 1, (hd - 1) % 2, l_prev)
        l_prev = softmax(s_cur, hd % 2)
    weighted_values(N_Q_HEADS - 1, (N_Q_HEADS - 1) % 2, l_prev)


def _attention(q, k, vt, *, tq):
    b, l, _ = q.shape
    t = k.shape[1]
    return pl.pallas_call(
        _attn_kernel,
        grid=(b, l // tq),
        in_specs=[
            pl.BlockSpec((None, tq, ATTN_W), lambda bi, i: (bi, i, 0)),
            pl.BlockSpec((None, t, KV_W), lambda bi, i: (bi, 0, 0)),
            pl.BlockSpec((None, KV_W, t), lambda bi, i: (bi, 0, 0)),
        ],
        out_specs=pl.BlockSpec((None, tq, ATTN_W), lambda bi, i: (bi, i, 0)),
        out_shape=jax.ShapeDtypeStruct((b, l, ATTN_W), BF16),
        scratch_shapes=[pltpu.VMEM((2, t, tq), BF16)],
        compiler_params=_cparams(("parallel", "parallel")),
        name="attention",
    )(q, k, vt)


def _dft_cos_sin(n):
    k = np.arange(n, dtype=np.float64)
    ang = 2.0 * np.pi * np.outer(k, k) / n
    return np.cos(ang), np.sin(ang)


def _fft_stage1_kernel(x_ref, w_ref, tc_ref, ts_ref, o_ref):
    l1 = x_ref.shape[0]
    half = l1 * SUBLANES
    x = x_ref[...].reshape(half, 2 * BRANCH_W)
    stacked = jnp.concatenate([x[:, :BRANCH_W], x[:, BRANCH_W:]], axis=0).astype(BF16)
    u = _dot(w_ref[...], stacked)
    tc = jnp.concatenate([tc_ref[...]] * N_GROUPS, axis=1)
    ts = jnp.concatenate([ts_ref[...]] * N_GROUPS, axis=1)
    ur, ui = u[:half], u[half:]
    o_ref[0] = (ur * tc + ui * ts).reshape(o_ref.shape[1:])
    o_ref[1] = (ui * tc - ur * ts).reshape(o_ref.shape[1:])


def _fft_stage2_kernel(z_ref, w_ref, o_ref, *, scale):
    z = z_ref[...].reshape(-1, BRANCH_W).astype(BF16)
    o_ref[...] = (_dot(w_ref[...], z) * scale).reshape(o_ref.shape)


def _fourier_latent(ftcs):
    b, s, _ = ftcs.shape
    sub, l2 = SUBLANES, FFT_L2
    l1 = s // l2
    assert l2 == sub * sub and l1 % sub == 0
    eye = np.eye(sub)
    c1, s1 = _dft_cos_sin(l1)
    c2, s2 = _dft_cos_sin(l2)
    w1 = _bf16_const(np.kron(np.block([[c1, -s1], [-s1, -c1]]), eye))
    cs2 = np.stack([c2, s2]).reshape(2, l2, sub, sub)
    w2 = _bf16_const(np.einsum("rkxy,lm->klxrmy", cs2, eye).reshape(l2 * sub, 2 * l2 * sub))
    bpos = np.arange(l2).reshape(sub, 1, sub)
    ang = 2.0 * np.pi * (np.arange(l1).reshape(1, l1, 1) * bpos) / s
    ang = np.repeat(ang.reshape(sub, l1 * sub, 1), GROUP_W, axis=2)
    tc, ts = jnp.asarray(np.cos(ang), F32), jnp.asarray(np.sin(ang), F32)
    x5 = ftcs.reshape(b, l1, sub, sub, 2 * BRANCH_W)
    z = pl.pallas_call(
        _fft_stage1_kernel,
        grid=(b, sub),
        in_specs=[
            pl.BlockSpec((None, l1, None, sub, 2 * BRANCH_W), lambda bi, i: (bi, 0, i, 0, 0)),
            _const_spec((2 * l1 * sub, 2 * l1 * sub)),
            pl.BlockSpec((None, l1 * sub, GROUP_W), lambda bi, i: (i, 0, 0)),
            pl.BlockSpec((None, l1 * sub, GROUP_W), lambda bi, i: (i, 0, 0)),
        ],
        out_specs=pl.BlockSpec((None, None, 2, l1 // sub, l2, BRANCH_W), lambda bi, i: (bi, i, 0, 0, 0, 0)),
        out_shape=jax.ShapeDtypeStruct((b, sub, 2, l1 // sub, l2, BRANCH_W), F32),
        compiler_params=_cparams(("parallel", "parallel")),
        name="fourier_stage1",
    )(x5, w1, tc, ts)
    y = pl.pallas_call(
        functools.partial(_fft_stage2_kernel, scale=1.0 / math.sqrt(s * GROUP_W)),
        grid=(b, l1 // sub),
        in_specs=[
            pl.BlockSpec((None, sub, 2, None, l2, BRANCH_W), lambda bi, i: (bi, 0, 0, i, 0, 0)),
            _const_spec((l2 * sub, 2 * l2 * sub)),
        ],
        out_specs=pl.BlockSpec((None, l2, None, sub, BRANCH_W), lambda bi, i: (bi, 0, i, 0, 0)),
        out_shape=jax.ShapeDtypeStruct((b, l2, l1 // sub, sub, BRANCH_W), F32),
        compiler_params=_cparams(("parallel", "parallel")),
        name="fourier_stage2",
    )(z, w2)
    return y.reshape(b, s, BRANCH_W)


def _fft_direct_kernel(x_ref, w_ref, o_ref, *, scale):
    x = x_ref[...]
    stacked = jnp.concatenate([x[:, :BRANCH_W], x[:, BRANCH_W:]], axis=0).astype(BF16)
    o_ref[...] = _dot(w_ref[...], stacked) * scale


def _fourier_direct(ftcs):
    b, l, _ = ftcs.shape
    c, s = _dft_cos_sin(l)
    w = _bf16_const(np.concatenate([c, -s], axis=1))
    return pl.pallas_call(
        functools.partial(_fft_direct_kernel, scale=1.0 / math.sqrt(l * GROUP_W)),
        grid=(b,),
        in_specs=[pl.BlockSpec((None, l, 2 * BRANCH_W), lambda bi: (bi, 0, 0)), _const_spec((l, 2 * l))],
        out_specs=pl.BlockSpec((None, l, BRANCH_W), lambda bi: (bi, 0, 0)),
        out_shape=jax.ShapeDtypeStruct((b, l, BRANCH_W), F32),
        compiler_params=_cparams(("parallel",)),
        name="fourier_direct",
    )(ftcs, w)


def _post_norm(x, y, gain, bias, alpha):
    return _normalize(alpha * x + y) * gain + bias


def _merge_kernel(ya_ref, ysg_ref, yft_ref, gate_ref, x_ref, m_ref, wb_ref, wo_ref, g_ref, b_ref, o_ref, *, d, alpha):
    acc = None
    for n, y_ref in enumerate((ya_ref, ysg_ref, yft_ref)):
        p = gate_ref[:, n * d:(n + 1) * d].astype(F32) * _dot(y_ref[...].astype(BF16), wb_ref[n])
        acc = p if acc is None else acc + p
    y = _dot(acc.astype(BF16), wo_ref[...])
    o_ref[...] = _post_norm(x_ref[...], m_ref[:, 2 * d:3 * d] * y, g_ref[...], b_ref[...], alpha)


def _merge(ya, ysg, yft, gate, x, mod, mod_row, wb, wo, gain, bias, *, tm, alpha):
    b, l, d = x.shape
    tok = lambda w: pl.BlockSpec((None, tm, w), lambda bi, i: (bi, i, 0))
    return pl.pallas_call(
        functools.partial(_merge_kernel, d=d, alpha=alpha),
        grid=(b, l // tm),
        in_specs=[tok(BRANCH_W), tok(BRANCH_W), tok(BRANCH_W), tok(N_BRANCH * d), tok(d),
                  _mod_spec(d, mod_row),
                  _const_spec((N_BRANCH, BRANCH_W, d)), _const_spec((d, d)),
                  _const_spec((1, d)), _const_spec((1, d))],
        out_specs=tok(d),
        out_shape=jax.ShapeDtypeStruct((b, l, d), F32),
        compiler_params=_cparams(("parallel", "parallel")),
        name="merge",
    )(ya, ysg, yft, gate, x, mod, wb, wo, gain, bias)


def _ffn_kernel(x_ref, m_ref, wg_ref, wu_ref, wd_ref, g_ref, b_ref, o_ref, *, d, alpha):
    x = x_ref[...]
    h = (_normalize(x) * (1.0 + m_ref[:, 4 * d:5 * d]) + m_ref[:, 3 * d:4 * d]).astype(BF16)
    gate = _dot(h, wg_ref[...])
    act = (gate * _sigmoid(gate) * _dot(h, wu_ref[...])).astype(BF16)
    f = _dot(act, wd_ref[...])
    o_ref[...] = _post_norm(x, m_ref[:, 5 * d:6 * d] * f, g_ref[...], b_ref[...], alpha)


def _dense_ffn(x, mod, mod_row, wg, wu, wd, gain, bias, *, tm, alpha):
    b, l, d = x.shape
    dff = wg.shape[1]
    tok = pl.BlockSpec((None, tm, d), lambda bi, i: (bi, i, 0))
    return pl.pallas_call(
        functools.partial(_ffn_kernel, d=d, alpha=alpha),
        grid=(b, l // tm),
        in_specs=[tok, _mod_spec(d, mod_row),
                  _const_spec((d, dff)), _const_spec((d, dff)), _const_spec((dff, d)),
                  _const_spec((1, d)), _const_spec((1, d))],
        out_specs=tok,
        out_shape=jax.ShapeDtypeStruct((b, l, d), F32),
        compiler_params=_cparams(("parallel", "parallel")),
        name="dense_ffn",
    )(x, mod, wg, wu, wd, gain, bias)


def _ffn_input(x, m_ref, d):
    return _normalize(x) * (1.0 + m_ref[:, 4 * d:5 * d]) + m_ref[:, 3 * d:4 * d]


def _router_kernel(x_ref, m_ref, rhi_ref, rlo_ref, tri_ref, idx_ref, wts_ref, rank_ref, cnt_ref, carry, *, d, n_exp):
    @pl.when((pl.program_id(0) == 0) & (pl.program_id(1) == 0))
    def _():
        carry[...] = jnp.zeros_like(carry)

    tm = x_ref.shape[0]
    hi, lo = _split_bf16(_ffn_input(x_ref[...], m_ref, d))
    logits = _dot_nt(rhi_ref[...], hi) + _dot_nt(rhi_ref[...], lo) + _dot_nt(rlo_ref[...], hi)
    eid = lax.broadcasted_iota(I32, (n_exp, tm), 0).astype(F32)
    v1 = jnp.max(logits, axis=0, keepdims=True)
    i1 = jnp.min(jnp.where(logits == v1, eid, float(n_exp)), axis=0, keepdims=True)
    sel1 = eid == i1
    rest = jnp.where(sel1, -jnp.inf, logits)
    v2 = jnp.max(rest, axis=0, keepdims=True)
    i2 = jnp.min(jnp.where(rest == v2, eid, float(n_exp)), axis=0, keepdims=True)
    sel2 = eid == i2
    w1 = 1.0 / (1.0 + jnp.exp(v2 - v1))
    onehot = jnp.where(sel1 | sel2, 1.0, 0.0)
    before = _dot(onehot.astype(BF16), tri_ref[...]) + carry[:, 0:1]
    idx_ref[0:1, :] = i1.astype(I32)
    idx_ref[1:2, :] = i2.astype(I32)
    wts_ref[...] = jnp.zeros_like(wts_ref)
    wts_ref[0:1, :] = w1
    wts_ref[1:2, :] = 1.0 - w1
    rank_ref[0:1, :] = jnp.sum(jnp.where(sel1, before, 0.0), axis=0, keepdims=True).astype(I32)
    rank_ref[1:2, :] = jnp.sum(jnp.where(sel2, before, 0.0), axis=0, keepdims=True).astype(I32)
    carry[...] = carry[...] + jnp.sum(onehot, axis=1, keepdims=True)
    cnt_ref[...] = carry[...]


def _route(x, mod, mod_row, router, *, tm):
    b, l, d = x.shape
    n_exp = router.shape[1]
    n = b * l
    rt = router.T
    rhi = rt.astype(BF16)
    rlo = (rt - rhi.astype(F32)).astype(BF16)
    tri = _bf16_const(np.triu(np.ones((tm, tm)), 1))
    per_b = l // tm
    flat = lambda bi, i: (0, bi * per_b + i)
    return pl.pallas_call(
        functools.partial(_router_kernel, d=d, n_exp=n_exp),
        grid=(b, per_b),
        in_specs=[pl.BlockSpec((None, tm, d), lambda bi, i: (bi, i, 0)),
                  _mod_spec(d, mod_row),
                  _const_spec((n_exp, d)), _const_spec((n_exp, d)), _const_spec((tm, tm))],
        out_specs=[pl.BlockSpec((2, tm), flat), pl.BlockSpec((8, tm), flat), pl.BlockSpec((2, tm), flat),
                   pl.BlockSpec((n_exp, LANES), lambda bi, i: (0, 0))],
        out_shape=[jax.ShapeDtypeStruct((2, n), I32), jax.ShapeDtypeStruct((8, n), F32),
                   jax.ShapeDtypeStruct((2, n), I32), jax.ShapeDtypeStruct((n_exp, LANES), F32)],
        scratch_shapes=[pltpu.VMEM((n_exp, LANES), F32)],
        compiler_params=_cparams(("arbitrary", "arbitrary")),
        name="router",
    )(x, mod, rhi, rlo, tri)


def _row_copy(src_ref, src_row, dst_ref, dst_row, sem):
    return pltpu.make_async_copy(src_ref.at[pl.ds(src_row, 1)], dst_ref.at[pl.ds(dst_row, 1)], sem)


def _scatter_kernel(pos_ref, x_ref, m_ref, init_ref, xs_ref, h_scr, sem, *, d):
    del init_ref
    tm = x_ref.shape[0]
    h_scr[...] = _ffn_input(x_ref[...], m_ref, d)

    def issue(t, carry):
        for k in range(2):
            _row_copy(h_scr, t, xs_ref, pos_ref[k, t], sem).start()
        return carry

    lax.fori_loop(0, tm, issue, 0, unroll=8)
    for _ in range(2):
        pltpu.make_async_copy(h_scr, xs_ref.at[pl.ds(0, tm)], sem).wait()


def _scatter_rows(pos, x, mod, mod_row, n_rows, *, tm):
    b, l, d = x.shape
    per_b = l // tm
    return pl.pallas_call(
        functools.partial(_scatter_kernel, d=d),
        grid=(b, per_b),
        in_specs=[pl.BlockSpec((2, tm), lambda bi, i: (0, bi * per_b + i), memory_space=pltpu.SMEM),
                  pl.BlockSpec((None, tm, d), lambda bi, i: (bi, i, 0)), _mod_spec(d, mod_row),
                  pl.BlockSpec(memory_space=pl.ANY)],
        out_specs=pl.BlockSpec(memory_space=pl.ANY),
        out_shape=jax.ShapeDtypeStruct((n_rows, d), F32),
        scratch_shapes=[pltpu.VMEM((tm, d), F32), pltpu.SemaphoreType.DMA(())],
        input_output_aliases={3: 0},
        compiler_params=_cparams(("arbitrary", "arbitrary")),
        name="scatter_rows",
    )(pos, x, mod, jnp.zeros((n_rows, d), F32))


def _expert_kernel(te_ref, nu_ref, x_ref, wg_ref, wu_ref, wd_ref, o_ref, h_scr):
    del te_ref
    i, j = pl.program_id(0), pl.program_id(1)

    @pl.when(j == 0)
    def _():
        o_ref[...] = jnp.zeros_like(o_ref)

    @pl.when(i < nu_ref[0])
    def _():
        @pl.when(j == 0)
        def _():
            h_scr[...] = x_ref[...].astype(BF16)

        h = h_scr[...]
        gate = _dot(h, wg_ref[...])
        act = (gate * _sigmoid(gate) * _dot(h, wu_ref[...])).astype(BF16)
        o_ref[...] += _dot(act, wd_ref[...])


def _expert_ffn(tile_expert, n_used, xs, wg, wu, wd, *, tm, tf):
    n_rows, w = xs.shape
    n_exp, d, dff = wg.shape
    tf = tf if dff % tf == 0 else dff
    live =lambda i, nu: jnp.minimum(i, nu[0] - 1)
    grid_spec = pltpu.PrefetchScalarGridSpec(
        num_scalar_prefetch=2,
        grid=(n_rows // tm, dff // tf),
        in_specs=[pl.BlockSpec((tm, w), lambda i, j, te, nu: (live(i, nu), 0)),
                  pl.BlockSpec((None, d, tf), lambda i, j, te, nu: (te[live(i, nu)], 0, jnp.where(i < nu[0], j, dff // tf - 1))),
                  pl.BlockSpec((None, d, tf), lambda i, j, te, nu: (te[live(i, nu)], 0, jnp.where(i < nu[0], j, dff // tf - 1))),
                  pl.BlockSpec((None, tf, d), lambda i, j, te, nu: (te[live(i, nu)], jnp.where(i < nu[0], j, dff // tf - 1), 0))],
        out_specs=pl.BlockSpec((tm, w), lambda i, j, te, nu: (i, 0)),
        scratch_shapes=[pltpu.VMEM((tm, d), BF16)],
    )
    return pl.pallas_call(
        _expert_kernel,
        grid_spec=grid_spec,
        out_shape=jax.ShapeDtypeStruct((n_rows, w), F32),
        compiler_params=_cparams(("arbitrary", "arbitrary")),
        name="expert_ffn",
    )(tile_expert, n_used, xs, wg, wu, wd)


def _combine_kernel(pos_ref, ys_ref, wts_ref, x_ref, m_ref, g_ref, b_ref, o_ref, buf, sem, *, d, alpha):
    tm = x_ref.shape[0]

    def issue(t, carry):
        for k in range(2):
            pltpu.make_async_copy(ys_ref.at[pl.ds(pos_ref[k, t], 1)], buf.at[k, pl.ds(t, 1)], sem).start()
        return carry

    lax.fori_loop(0, tm, issue, 0, unroll=8)
    for k in range(2):
        pltpu.make_async_copy(ys_ref.at[pl.ds(0, tm)], buf.at[k], sem).wait()
    w = wts_ref[...].T
    f = w[:, 0:1] * buf[0] + w[:, 1:2] * buf[1]
    o_ref[...] = _post_norm(x_ref[...], m_ref[:, 5 * d:6 * d] * f, g_ref[...], b_ref[...], alpha)


def _combine(pos, ys, wts, x, mod, mod_row, gain, bias, *, tm, alpha):
    b, l, d = x.shape
    per_b = l // tm
    flat = lambda bi, i: (0, bi * per_b + i)
    tok = pl.BlockSpec((None, tm, d), lambda bi, i: (bi, i, 0))
    return pl.pallas_call(
        functools.partial(_combine_kernel, d=d, alpha=alpha),
        grid=(b, per_b),
        in_specs=[pl.BlockSpec((2, tm), flat, memory_space=pltpu.SMEM),
                  pl.BlockSpec(memory_space=pl.ANY),
                  pl.BlockSpec((8, tm), flat), tok,
                  _mod_spec(d, mod_row),
                  _const_spec((1, d)), _const_spec((1, d))],
        out_specs=tok,
        out_shape=jax.ShapeDtypeStruct((b, l, d), F32),
        scratch_shapes=[pltpu.VMEM((2, tm, d), F32), pltpu.SemaphoreType.DMA(())],
        compiler_params=_cparams(("arbitrary", "arbitrary")),
        name="combine",
    )(pos, ys, wts, x, mod, gain, bias)


def _moe_ffn(x, mod, mod_row, router, wg, wu, wd, gain, bias, *, tm, tm_e, tf, alpha):
    b, l, d = x.shape
    n = b * l
    n_exp = router.shape[1]
    idx, wts, rank, counts = _route(x, mod, mod_row, router, tm=tm)
    cnt = counts[:, 0].astype(I32)
    tiles = (cnt + tm_e - 1) // tm_e
    tile_end = jnp.cumsum(tiles)
    offsets = (tile_end - tiles) * tm_e
    eid = jnp.arange(n_exp, dtype=I32)
    pos = rank + jnp.sum(jnp.where(idx[None] == eid[:, None, None], offsets[:, None, None], 0), axis=0)
    n_tiles = (2 * n) // tm_e + n_exp
    tile_ids = jnp.arange(n_tiles, dtype=I32)
    tile_expert = jnp.minimum(jnp.sum((tile_ids[:, None] >= tile_end[None, :]).astype(I32), axis=1), n_exp - 1)
    n_used = tile_end[-1:].astype(I32)
    xs = _scatter_rows(pos, x, mod, mod_row, n_tiles * tm_e, tm=tm)
    ys = _expert_ffn(tile_expert, n_used, xs, wg, wu, wd, tm=tm_e, tf=tf)
    return _combine(pos, ys, wts, x, mod, mod_row, gain, bias, tm=tm, alpha=alpha)


def _token_mixer_consts(q_norm, k_norm, sg_w, sg_b):
    qg = (jnp.tile(q_norm, N_Q_HEADS) * (HEAD_DIM ** -0.5 * math.log2(math.e))).reshape(1, ATTN_W)
    kg = jnp.tile(k_norm, N_KV_HEADS).reshape(1, KV_W)
    sgb = jnp.broadcast_to(sg_b[:, :, None], (N_GROUPS, CHUNK, GROUP_W))
    return qg, kg, sg_w.astype(BF16), sgb


def kernel(x, c, ctx, c_ctx, w_mod, b_mod, w_in, q_norm, k_norm, sg_w, sg_b, w_branch, w_out, ln1_g, ln1_b, ln2_g, ln2_b, ffn_w_gate, ffn_w_up, ffn_w_down, router, exp_w_gate, exp_w_up, exp_w_down):
    b, s, d = x.shape
    cl = ctx.shape[1]
    depth = w_in.shape[0]
    assert s % (GRID_W * FFT_L2) == 0 and s % 512 == 0 and cl % CHUNK == 0 and b + 1 <= MOD_ROWS
    alpha = (2 * depth) ** 0.25

    cond = jnp.zeros((MOD_ROWS, d), F32).at[:b].set(c).at[b].set(c_ctx)
    mod = _modulation(cond, w_mod, b_mod).reshape(depth, MOD_ROWS, 1, 6 * d)

    t = np.arange(s)
    lat_tables = _rope_tables(t // GRID_W, t % GRID_W)
    ctx_tables = _rope_tables(np.zeros(cl), np.zeros(cl))
    head = np.arange(ATTN_W) // HEAD_DIM
    gmat = _bf16_const(head[:, None] == head[None, :])
    cc, sc = _dft_cos_sin(GROUP_W)
    dftc = _bf16_const(np.concatenate([cc, sc], axis=1))
    row2 = lambda v: v.reshape(1, d)

    x_lat, x_ctx = x, ctx
    for l in range(depth):
        last = l == depth - 1
        m = mod[l]
        consts = _token_mixer_consts(q_norm[l], k_norm[l], sg_w[l], sg_b[l])
        w_in_l = w_in[l].astype(BF16)
        wb, wo = w_branch[l].astype(BF16), w_out[l].astype(BF16)
        proj = functools.partial(_in_projection, w_in=w_in_l, qg=consts[0], kg=consts[1], gmat=gmat,
                                 sgw=consts[2], sgb=consts[3], dftc=dftc)
        q, k, vt, ysg, ftcs, gate = proj(x_lat, m, None, tables=lat_tables, tm=512, kv_only=False)
        ctx_out = proj(x_ctx, m, b, tables=ctx_tables, tm=cl, kv_only=last)
        k_c, vt_c = ctx_out[:2] if last else ctx_out[1:3]
        ya = _attention(q, jnp.concatenate([k_c, k], axis=1), jnp.concatenate([vt_c, vt], axis=2), tq=256)
        yft = _fourier_latent(ftcs)
        x_lat = _merge(ya, ysg, yft, gate, x_lat, m, None, wb, wo, row2(ln1_g[l]), row2(ln1_b[l]), tm=512, alpha=alpha)
        if not last:
            q_c, _, _, ysg_c, ftcs_c, gate_c = ctx_out
            ya_c = _attention(q_c, k_c, vt_c, tq=cl)
            x_ctx = _merge(ya_c, ysg_c, _fourier_direct(ftcs_c), gate_c, x_ctx, m, b, wb, wo,
                           row2(ln1_g[l]), row2(ln1_b[l]), tm=cl, alpha=alpha)
        i = l // 2
        if l % 2 == 0:
            wg, wu, wd = ffn_w_gate[i].astype(BF16), ffn_w_up[i].astype(BF16), ffn_w_down[i].astype(BF16)
            ffn = functools.partial(_dense_ffn, wg=wg, wu=wu, wd=wd, gain=row2(ln2_g[l]), bias=row2(ln2_b[l]), alpha=alpha)
            x_lat = ffn(x_lat, m, None, tm=512)
            if not last:
                x_ctx = ffn(x_ctx, m, b, tm=cl)
        else:
            moe = functools.partial(_moe_ffn, router=router[i], wg=exp_w_gate[i].astype(BF16),
                                    wu=exp_w_up[i].astype(BF16), wd=exp_w_down[i].astype(BF16),
                                    gain=row2(ln2_g[l]), bias=row2(ln2_b[l]), tm_e=512, tf=1792, alpha=alpha)
            x_lat = moe(x_lat, m, None, tm=512)
            if not last:
                x_ctx = moe(x_ctx, m, b, tm=cl)
    return x_lat
```

```python
import functools
import math

import numpy as np
import jax
import jax.numpy as jnp
from jax import lax
from jax.experimental import pallas as pl
from jax.experimental.pallas import tpu as pltpu

F32 = jnp.float32
BF16 = jnp.bfloat16
I32 = jnp.int32

N_Q_HEADS = 8
N_KV_HEADS = 2
GQA_GROUP = N_Q_HEADS // N_KV_HEADS
HEAD_DIM = 64
ATTN_W = N_Q_HEADS * HEAD_DIM
KV_W = N_KV_HEADS * HEAD_DIM
GRID_W = 64
ROPE_THETA = 10000.0
CHUNK = 128
N_GROUPS = 4
GROUP_W = 128
BRANCH_W = N_GROUPS * GROUP_W
N_BRANCH = 3
OFF_Q = 0
OFF_K = OFF_Q + ATTN_W
OFF_V = OFF_K + KV_W
OFF_U = OFF_V + KV_W
OFF_SGV = OFF_U + BRANCH_W
OFF_FT = OFF_SGV + BRANCH_W
OFF_GATE = OFF_FT + BRANCH_W
LN_EPS = 1e-6
RMS_EPS = 1e-6
GELU_C = math.sqrt(2.0 / math.pi)

LANES = 128
V7X_VMEM_BYTES = 64 * 1024 * 1024
VMEM_LIMIT = V7X_VMEM_BYTES - 8 * 1024 * 1024
SUBLANES = 8
FFT_L2 = SUBLANES * SUBLANES
MOD_ROWS = 16
SCORE_BOUND = 100.0
SCORE_BOUND_MARGIN = 1.0 + 2.0 ** -6


def _cparams(sem):
    return pltpu.CompilerParams(dimension_semantics=sem, vmem_limit_bytes=VMEM_LIMIT)


def _const_spec(shape):
    nd = len(shape)
    return pl.BlockSpec(shape, lambda *_: (0,) * nd, pipeline_mode=pl.Buffered(1))


def _mod_spec(d, mod_row):
    if mod_row is None:
        return pl.BlockSpec((None, 1, 6 * d), lambda bi, i: (bi, 0, 0))
    return pl.BlockSpec((None, 1, 6 * d), lambda bi, i: (mod_row, 0, 0))


def _bf16_const(a):
    return jnp.asarray(a, F32).astype(BF16)


def _dot(a, b):
    return jnp.dot(a, b, preferred_element_type=F32)


def _dot_nt(a, b):
    return lax.dot_general(a, b, (((1,), (1,)), ((), ())), preferred_element_type=F32)


def _sigmoid(x):
    return 0.5 * jnp.tanh(0.5 * x) + 0.5


def _gelu_tanh(x):
    return 0.5 * x * (1.0 + jnp.tanh(GELU_C * (x + 0.044715 * (x * x * x))))


def _normalize(x):
    mu = jnp.mean(x, axis=-1, keepdims=True)
    xc = x - mu
    var = jnp.mean(xc * xc, axis=-1, keepdims=True)
    return xc * lax.rsqrt(var + LN_EPS)


def _split_bf16(x):
    hi = x.astype(BF16)
    lo = (x - hi.astype(F32)).astype(BF16)
    return hi, lo


def _mod_kernel(c_ref, w_ref, b_ref, o_ref):
    c = c_ref[...]
    s = c * _sigmoid(c)
    o_ref[...] = _dot(s.astype(BF16), w_ref[...].astype(BF16)) + b_ref[...]


def _modulation(cond, w_mod, b_mod):
    depth, d, d6 = w_mod.shape
    tn = 2 * d if d6 % (2 * d) == 0 else d
    return pl.pallas_call(
        _mod_kernel,
        grid=(depth, d6 // tn),
        in_specs=[
            pl.BlockSpec((MOD_ROWS, d), lambda l, j: (0, 0)),
            pl.BlockSpec((None, d, tn), lambda l, j: (l, 0, j)),
            pl.BlockSpec((None, 1, tn), lambda l, j: (l, 0, j)),
        ],
        out_specs=pl.BlockSpec((None, MOD_ROWS, tn), lambda l, j: (l, 0, j)),
        out_shape=jax.ShapeDtypeStruct((depth, MOD_ROWS, d6), F32),
        compiler_params=_cparams(("parallel", "parallel")),
        name="modulation",
    )(cond, w_mod, b_mod.reshape(depth, 1, d6))


def _rope_tables(pos_row, pos_col):
    half = HEAD_DIM // 2
    d2 = half // 2
    lane = np.arange(LANES) % HEAD_DIM
    inv = ROPE_THETA ** (-(lane % d2).astype(np.float64) / d2)
    pos = np.where((lane < half)[None, :], pos_row[:, None], pos_col[:, None]).astype(np.float64)
    ang = pos * inv[None, :]
    first = ((lane % half) < d2)[None, :]
    cos = np.cos(ang)
    sin = np.sin(ang)
    sin_a = np.where(first, -sin, 0.0)
    sin_b = np.where(first, 0.0, sin)
    return (jnp.asarray(cos, F32), jnp.asarray(sin_a, F32), jnp.asarray(sin_b, F32))


def _rope(x, cos, sin_a, sin_b):
    d2 = HEAD_DIM // 4
    return x * cos + pltpu.roll(x, LANES - d2, 1) * sin_a + pltpu.roll(x, d2, 1) * sin_b


def _head_rms(z, gmat, gain):
    hi, lo = _split_bf16(z * z)
    ss = _dot(hi, gmat) + _dot(lo, gmat)
    return z * lax.rsqrt(ss * (1.0 / HEAD_DIM) + RMS_EPS) * gain


def _inproj_kernel(x_ref, m_ref, w_ref, cos_ref, sa_ref, sb_ref, qg_ref, kg_ref, gm_ref, sgw_ref, sgb_ref,
                   dft_ref, *out_refs, d, kv_only):
    tm = x_ref.shape[0]
    h = (_normalize(x_ref[...]) * (1.0 + m_ref[:, d:2 * d]) + m_ref[:, 0:d]).astype(BF16)
    cos, sin_a, sin_b = cos_ref[...], sa_ref[...], sb_ref[...]

    def proj(a, b):
        return _dot(h, w_ref[:, a:b])

    if kv_only:
        k_ref, vt_ref = out_refs
    else:
        q_ref, k_ref, vt_ref, ysg_ref, ft_ref, gate_ref = out_refs

    def finish_q(z):
        qn = _head_rms(z, gm_ref[...], qg_ref[...])
        for s in range(ATTN_W // LANES):
            sl = slice(s * LANES, (s + 1) * LANES)
            q_ref[:, sl] = _rope(qn[:, sl], cos, sin_a, sin_b).astype(BF16)

    def finish_kv(z):
        kn = _head_rms(z[:, :KV_W], gm_ref[0:KV_W, 0:KV_W], kg_ref[...])
        k_ref[...] = _rope(kn, cos, sin_a, sin_b).astype(BF16)
        vt_ref[...] = z[:, KV_W:].T.astype(BF16)

    def finish_gating(z):
        u = _gelu_tanh(z[:, :BRANCH_W])
        v2 = _gelu_tanh(z[:, BRANCH_W:])
        for g in range(N_GROUPS):
            gl = slice(g * GROUP_W, (g + 1) * GROUP_W)
            vg = _normalize(v2[:, gl]).astype(BF16)
            for c in range(tm // CHUNK):
                rows = slice(c * CHUNK, (c + 1) * CHUNK)
                mixed = _dot(sgw_ref[g], vg[rows, :]) + sgb_ref[g]
                ysg_ref[rows, gl] = (u[rows, gl] * mixed).astype(BF16)

    def finish_fourier(z):
        zf = z.astype(BF16)
        for g in range(N_GROUPS):
            cs = _dot(zf[:, g * GROUP_W:(g + 1) * GROUP_W], dft_ref[...])
            ft_ref[:, g * GROUP_W:(g + 1) * GROUP_W] = cs[:, :GROUP_W]
            ft_ref[:, BRANCH_W + g * GROUP_W:BRANCH_W + (g + 1) * GROUP_W] = cs[:, GROUP_W:]

    def finish_gate(n):
        def store(z):
            gate_ref[:, n * d:(n + 1) * d] = _sigmoid(z).astype(BF16)
        return store

    if kv_only:
        finish_kv(proj(OFF_K, OFF_U))
        return
    stages = [((OFF_Q, OFF_K), finish_q), ((OFF_K, OFF_U), finish_kv), ((OFF_U, OFF_FT), finish_gating),
              ((OFF_FT, OFF_GATE), finish_fourier)]
    stages += [((OFF_GATE + n * d, OFF_GATE + (n + 1) * d), finish_gate(n)) for n in range(N_BRANCH)]
    z = proj(*stages[0][0])
    for i, (_, finish) in enumerate(stages):
        z_next = proj(*stages[i + 1][0]) if i + 1 < len(stages) else None
        finish(z)
        z = z_next


def _in_projection(x, mod, mod_row, w_in, tables, qg, kg, gmat, sgw, sgb, dftc, *, tm, kv_only):
    b, l, d = x.shape
    in_w = w_in.shape[1]
    cos, sin_a, sin_b = tables
    tok = lambda w: pl.BlockSpec((None, tm, w), lambda bi, i: (bi, i, 0))
    tab = pl.BlockSpec((tm, LANES), lambda bi, i: (i, 0))
    out_specs = [tok(KV_W), pl.BlockSpec((None, KV_W, tm), lambda bi, i: (bi, 0, i))]
    out_shape = [jax.ShapeDtypeStruct((b, l, KV_W), BF16), jax.ShapeDtypeStruct((b, KV_W, l), BF16)]
    if not kv_only:
        out_specs = [tok(ATTN_W)] + out_specs + [tok(BRANCH_W), tok(2 * BRANCH_W), tok(N_BRANCH * d)]
        out_shape = ([jax.ShapeDtypeStruct((b, l, ATTN_W), BF16)] + out_shape
                     + [jax.ShapeDtypeStruct((b, l, BRANCH_W), BF16),
                        jax.ShapeDtypeStruct((b, l, 2 * BRANCH_W), F32),
                        jax.ShapeDtypeStruct((b, l, N_BRANCH * d), BF16)])
    return pl.pallas_call(
        functools.partial(_inproj_kernel, d=d, kv_only=kv_only),
        grid=(b, l // tm),
        in_specs=[
            tok(d),
            _mod_spec(d, mod_row),
            _const_spec((d, in_w)),
            tab, tab, tab,
            _const_spec((1, ATTN_W)), _const_spec((1, KV_W)), _const_spec((ATTN_W, ATTN_W)),
            _const_spec((N_GROUPS, CHUNK, CHUNK)), _const_spec((N_GROUPS, CHUNK, GROUP_W)),
            _const_spec((GROUP_W, 2 * GROUP_W)),
        ],
        out_specs=out_specs,
        out_shape=out_shape,
        compiler_params=_cparams(("parallel", "parallel")),
        name="in_projection_kv" if kv_only else "in_projection",
    )(x, mod, w_in, cos, sin_a, sin_b, qg, kg, gmat, sgw, sgb, dftc)


def _attn_kernel(q_ref, k_ref, vt_ref, o_ref, p_scr, *, bounded):
    t = k_ref.shape[0]

    def q_transposed(hd):
        return q_ref[:, hd * HEAD_DIM:(hd + 1) * HEAD_DIM].astype(F32).T.astype(BF16)

    def weighted_values(hd, slot, l):
        kv = hd // GQA_GROUP
        o = _dot(vt_ref[kv * HEAD_DIM:(kv + 1) * HEAD_DIM, :], p_scr[slot]) * (1.0 / l)
        o_ref[:, hd * HEAD_DIM:(hd + 1) * HEAD_DIM] = o.T.astype(BF16)

    if bounded:
        def probabilities(hd, slot):
            kv = hd // GQA_GROUP
            qt = q_transposed(hd)
            lacc = None
            for j in range(t // LANES):
                rows = slice(j * LANES, (j + 1) * LANES)
                p = jnp.exp2(_dot(k_ref[rows, kv * HEAD_DIM:(kv + 1) * HEAD_DIM], qt))
                lacc = p if lacc is None else lacc + p
                p_scr[slot, rows, :] = p.astype(BF16)
            return jnp.sum(lacc, axis=0, keepdims=True)

        l_prev = None
        for hd in range(N_Q_HEADS):
            l_cur = probabilities(hd, hd % 2)
            if hd >= 1:
                weighted_values(hd - 1, (hd - 1) % 2, l_prev)
            l_prev = l_cur
        weighted_values(N_Q_HEADS - 1, (N_Q_HEADS - 1) % 2, l_prev)
        return

    rb = 256 if t % 256 == 0 else LANES
    nb = t // rb

    def scores(hd):
        kv = hd // GQA_GROUP
        return _dot(k_ref[:, kv * HEAD_DIM:(kv + 1) * HEAD_DIM], q_transposed(hd))

    def softmax(s, slot):
        macc = s[0:rb, :]
        for j in range(1, nb):
            macc = jnp.maximum(macc, s[j * rb:(j + 1) * rb, :])
        m = jnp.max(macc, axis=0, keepdims=True)
        lacc = None
        for j in range(nb):
            p = jnp.exp2(s[j * rb:(j + 1) * rb, :] - m)
            lacc = p if lacc is None else lacc + p
            p_scr[slot, j * rb:(j + 1) * rb, :] = p.astype(BF16)
        return jnp.sum(lacc, axis=0, keepdims=True)

    s_next = scores(0)
    l_prev = None
    for hd in range(N_Q_HEADS):
        s_cur = s_next
        if hd + 1 < N_Q_HEADS:
            s_next = scores(hd + 1)
        if hd >= 1:
            weighted_values(hd - 1, (hd - 1) % 2, l_prev)
        l_prev = softmax(s_cur, hd % 2)
    weighted_values(N_Q_HEADS - 1, (N_Q_HEADS - 1) % 2, l_prev)


def _attention_call(q, k, vt, *, tq, bounded):
    b, l, _ = q.shape
    t = k.shape[1]
    return pl.pallas_call(
        functools.partial(_attn_kernel, bounded=bounded),
        grid=(b, l // tq),
        in_specs=[
            pl.BlockSpec((None, tq, ATTN_W), lambda bi, i: (bi, i, 0)),
            pl.BlockSpec((None, t, KV_W), lambda bi, i: (bi, 0, 0)),
            pl.BlockSpec((None, KV_W, t), lambda bi, i: (bi, 0, 0)),
        ],
        out_specs=pl.BlockSpec((None, tq, ATTN_W), lambda bi, i: (bi, i, 0)),
        out_shape=jax.ShapeDtypeStruct((b, l, ATTN_W), BF16),
        scratch_shapes=[pltpu.VMEM((2, t, tq), BF16)],
        compiler_params=_cparams(("parallel", "parallel")),
        name="attention_bounded" if bounded else "attention",
    )(q, k, vt)


def _max_head_norm(z):
    zf = z.astype(F32).reshape(z.shape[:-1] + (z.shape[-1] // HEAD_DIM, HEAD_DIM))
    return jnp.sqrt(jnp.max(jnp.sum(zf * zf, axis=-1)))


def _attention(q, k, vt, *, tq):
    bound = _max_head_norm(q) * _max_head_norm(k) * SCORE_BOUND_MARGIN
    return lax.cond(bound <= SCORE_BOUND,
                    functools.partial(_attention_call, tq=tq, bounded=True),
                    functools.partial(_attention_call, tq=tq, bounded=False), q, k, vt)


def _dft_cos_sin(n):
    k = np.arange(n, dtype=np.float64)
    ang = 2.0 * np.pi * np.outer(k, k) / n
    return np.cos(ang), np.sin(ang)


def _fft_stage1_kernel(x_ref, w_ref, tc_ref, ts_ref, o_ref):
    l1 = x_ref.shape[0]
    half = l1 * SUBLANES
    x = x_ref[...].reshape(half, 2 * BRANCH_W)
    stacked = jnp.concatenate([x[:, :BRANCH_W], x[:, BRANCH_W:]], axis=0).astype(BF16)
    u = _dot(w_ref[...], stacked)
    tc = jnp.concatenate([tc_ref[...]] * N_GROUPS, axis=1)
    ts = jnp.concatenate([ts_ref[...]] * N_GROUPS, axis=1)
    ur, ui = u[:half], u[half:]
    o_ref[0] = (ur * tc + ui * ts).reshape(o_ref.shape[1:])
    o_ref[1] = (ui * tc - ur * ts).reshape(o_ref.shape[1:])


def _fft_stage2_kernel(z_ref, w_ref, o_ref, *, scale):
    z = z_ref[...].reshape(-1, BRANCH_W).astype(BF16)
    o_ref[...] = (_dot(w_ref[...], z) * scale).reshape(o_ref.shape)


def _fourier_latent(ftcs):
    b, s, _ = ftcs.shape
    sub, l2 = SUBLANES, FFT_L2
    l1 = s // l2
    assert l2 == sub * sub and l1 % sub == 0
    eye = np.eye(sub)
    c1, s1 = _dft_cos_sin(l1)
    c2, s2 = _dft_cos_sin(l2)
    w1 = _bf16_const(np.kron(np.block([[c1, -s1], [-s1, -c1]]), eye))
    cs2 = np.stack([c2, s2]).reshape(2, l2, sub, sub)
    w2 = _bf16_const(np.einsum("rkxy,lm->klxrmy", cs2, eye).reshape(l2 * sub, 2 * l2 * sub))
    bpos = np.arange(l2).reshape(sub, 1, sub)
    ang = 2.0 * np.pi * (np.arange(l1).reshape(1, l1, 1) * bpos) / s
    ang = np.repeat(ang.reshape(sub, l1 * sub, 1), GROUP_W, axis=2)
    tc, ts = jnp.asarray(np.cos(ang), F32), jnp.asarray(np.sin(ang), F32)
    x5 = ftcs.reshape(b, l1, sub, sub, 2 * BRANCH_W)
    z = pl.pallas_call(
        _fft_stage1_kernel,
        grid=(b, sub),
        in_specs=[
            pl.BlockSpec((None, l1, None, sub, 2 * BRANCH_W), lambda bi, i: (bi, 0, i, 0, 0)),
            _const_spec((2 * l1 * sub, 2 * l1 * sub)),
            pl.BlockSpec((None, l1 * sub, GROUP_W), lambda bi, i: (i, 0, 0)),
            pl.BlockSpec((None, l1 * sub, GROUP_W), lambda bi, i: (i, 0, 0)),
        ],
        out_specs=pl.BlockSpec((None, None, 2, l1 // sub, l2, BRANCH_W), lambda bi, i: (bi, i, 0, 0, 0, 0)),
        out_shape=jax.ShapeDtypeStruct((b, sub, 2, l1 // sub, l2, BRANCH_W), F32),
        compiler_params=_cparams(("parallel", "parallel")),
        name="fourier_stage1",
    )(x5, w1, tc, ts)
    y = pl.pallas_call(
        functools.partial(_fft_stage2_kernel, scale=1.0 / math.sqrt(s * GROUP_W)),
        grid=(b, l1 // sub),
        in_specs=[
            pl.BlockSpec((None, sub, 2, None, l2, BRANCH_W), lambda bi, i: (bi, 0, 0, i, 0, 0)),
            _const_spec((l2 * sub, 2 * l2 * sub)),
        ],
        out_specs=pl.BlockSpec((None, l2, None, sub, BRANCH_W), lambda bi, i: (bi, 0, i, 0, 0)),
        out_shape=jax.ShapeDtypeStruct((b, l2, l1 // sub, sub, BRANCH_W), F32),
        compiler_params=_cparams(("parallel", "parallel")),
        name="fourier_stage2",
    )(z, w2)
    return y.reshape(b, s, BRANCH_W)


def _fft_direct_kernel(x_ref, w_ref, o_ref, *, scale):
    x = x_ref[...]
    stacked = jnp.concatenate([x[:, :BRANCH_W], x[:, BRANCH_W:]], axis=0).astype(BF16)
    o_ref[...] = _dot(w_ref[...], stacked) * scale


def _fourier_direct(ftcs):
    b, l, _ = ftcs.shape
    c, s = _dft_cos_sin(l)
    w = _bf16_const(np.concatenate([c, -s], axis=1))
    return pl.pallas_call(
        functools.partial(_fft_direct_kernel, scale=1.0 / math.sqrt(l * GROUP_W)),
        grid=(b,),
        in_specs=[pl.BlockSpec((None, l, 2 * BRANCH_W), lambda bi: (bi, 0, 0)), _const_spec((l, 2 * l))],
        out_specs=pl.BlockSpec((None, l, BRANCH_W), lambda bi: (bi, 0, 0)),
        out_shape=jax.ShapeDtypeStruct((b, l, BRANCH_W), F32),
        compiler_params=_cparams(("parallel",)),
        name="fourier_direct",
    )(ftcs, w)


def _post_norm(x, y, gain, bias, alpha):
    return _normalize(alpha * x + y) * gain + bias


def _merge_kernel(ya_ref, ysg_ref, yft_ref, gate_ref, x_ref, m_ref, wb_ref, wo_ref, g_ref, b_ref, o_ref, *, d, alpha):
    acc = None
    for n, y_ref in enumerate((ya_ref, ysg_ref, yft_ref)):
        p = gate_ref[:, n * d:(n + 1) * d].astype(F32) * _dot(y_ref[...].astype(BF16), wb_ref[n])
        acc = p if acc is None else acc + p
    y = _dot(acc.astype(BF16), wo_ref[...])
    o_ref[...] = _post_norm(x_ref[...], m_ref[:, 2 * d:3 * d] * y, g_ref[...], b_ref[...], alpha)


def _merge(ya, ysg, yft, gate, x, mod, mod_row, wb, wo, gain, bias, *, tm, alpha):
    b, l, d = x.shape
    tok = lambda w: pl.BlockSpec((None, tm, w), lambda bi, i: (bi, i, 0))
    return pl.pallas_call(
        functools.partial(_merge_kernel, d=d, alpha=alpha),
        grid=(b, l // tm),
        in_specs=[tok(BRANCH_W), tok(BRANCH_W), tok(BRANCH_W), tok(N_BRANCH * d), tok(d),
                  _mod_spec(d, mod_row),
                  _const_spec((N_BRANCH, BRANCH_W, d)), _const_spec((d, d)),
                  _const_spec((1, d)), _const_spec((1, d))],
        out_specs=tok(d),
        out_shape=jax.ShapeDtypeStruct((b, l, d), F32),
        compiler_params=_cparams(("parallel", "parallel")),
        name="merge",
    )(ya, ysg, yft, gate, x, mod, wb, wo, gain, bias)


def _swiglu(h, wg_ref, wu_ref, wd_ref):
    gate = _dot(h, wg_ref[...])
    act = (gate * _sigmoid(gate) * _dot(h, wu_ref[...])).astype(BF16)
    return _dot(act, wd_ref[...])


def _ffn_kernel(x_ref, m_ref, wg_ref, wu_ref, wd_ref, g_ref, b_ref, o_ref, *, d, alpha):
    x = x_ref[...]
    h = _ffn_input(x, m_ref, d).astype(BF16)
    f = _swiglu(h, wg_ref, wu_ref, wd_ref)
    o_ref[...] = _post_norm(x, m_ref[:, 5 * d:6 * d] * f, g_ref[...], b_ref[...], alpha)


def _dense_ffn(x, mod, mod_row, wg, wu, wd, gain, bias, *, tm, alpha):
    b, l, d = x.shape
    dff = wg.shape[1]
    tok = pl.BlockSpec((None, tm, d), lambda bi, i: (bi, i, 0))
    return pl.pallas_call(
        functools.partial(_ffn_kernel, d=d, alpha=alpha),
        grid=(b, l // tm),
        in_specs=[tok, _mod_spec(d, mod_row),
                  _const_spec((d, dff)), _const_spec((d, dff)), _const_spec((dff, d)),
                  _const_spec((1, d)), _const_spec((1, d))],
        out_specs=tok,
        out_shape=jax.ShapeDtypeStruct((b, l, d), F32),
        compiler_params=_cparams(("parallel", "parallel")),
        name="dense_ffn",
    )(x, mod, wg, wu, wd, gain, bias)


def _ffn_input(x, m_ref, d):
    return _normalize(x) * (1.0 + m_ref[:, 4 * d:5 * d]) + m_ref[:, 3 * d:4 * d]


def _router_kernel(x_ref, m_ref, rhi_ref, rlo_ref, tri_ref, idx_ref, wts_ref, rank_ref, cnt_ref, carry, *, d, n_exp):
    @pl.when((pl.program_id(0) == 0) & (pl.program_id(1) == 0))
    def _():
        carry[...] = jnp.zeros_like(carry)

    tm = x_ref.shape[0]
    hi, lo = _split_bf16(_ffn_input(x_ref[...], m_ref, d))
    logits = _dot_nt(rhi_ref[...], hi) + _dot_nt(rhi_ref[...], lo) + _dot_nt(rlo_ref[...], hi)
    eid = lax.broadcasted_iota(I32, (n_exp, tm), 0).astype(F32)
    v1 = jnp.max(logits, axis=0, keepdims=True)
    i1 = jnp.min(jnp.where(logits == v1, eid, float(n_exp)), axis=0, keepdims=True)
    sel1 = eid == i1
    rest = jnp.where(sel1, -jnp.inf, logits)
    v2 = jnp.max(rest, axis=0, keepdims=True)
    i2 = jnp.min(jnp.where(rest == v2, eid, float(n_exp)), axis=0, keepdims=True)
    sel2 = eid == i2
    w1 = 1.0 / (1.0 + jnp.exp(v2 - v1))
    onehot = jnp.where(sel1 | sel2, 1.0, 0.0)
    before = _dot(onehot.astype(BF16), tri_ref[...]) + carry[:, 0:1]
    idx_ref[0:1, :] = i1.astype(I32)
    idx_ref[1:2, :] = i2.astype(I32)
    wts_ref[...] = jnp.zeros_like(wts_ref)
    wts_ref[0:1, :] = w1
    wts_ref[1:2, :] = 1.0 - w1
    rank_ref[0:1, :] = jnp.sum(jnp.where(sel1, before, 0.0), axis=0, keepdims=True).astype(I32)
    rank_ref[1:2, :] = jnp.sum(jnp.where(sel2, before, 0.0), axis=0, keepdims=True).astype(I32)
    carry[...] = carry[...] + jnp.sum(onehot, axis=1, keepdims=True)
    cnt_ref[...] = carry[...]


def _route(x, mod, mod_row, router, *, tm):
    b, l, d = x.shape
    n_exp = router.shape[1]
    n = b * l
    rt = router.T
    rhi = rt.astype(BF16)
    rlo = (rt - rhi.astype(F32)).astype(BF16)
    tri = _bf16_const(np.triu(np.ones((tm, tm)), 1))
    per_b = l // tm
    flat = lambda bi, i: (0, bi * per_b + i)
    return pl.pallas_call(
        functools.partial(_router_kernel, d=d, n_exp=n_exp),
        grid=(b, per_b),
        in_specs=[pl.BlockSpec((None, tm, d), lambda bi, i: (bi, i, 0)),
                  _mod_spec(d, mod_row),
                  _const_spec((n_exp, d)), _const_spec((n_exp, d)), _const_spec((tm, tm))],
        out_specs=[pl.BlockSpec((2, tm), flat), pl.BlockSpec((8, tm), flat), pl.BlockSpec((2, tm), flat),
                   pl.BlockSpec((n_exp, LANES), lambda bi, i: (0, 0))],
        out_shape=[jax.ShapeDtypeStruct((2, n), I32), jax.ShapeDtypeStruct((8, n), F32),
                   jax.ShapeDtypeStruct((2, n), I32), jax.ShapeDtypeStruct((n_exp, LANES), F32)],
        scratch_shapes=[pltpu.VMEM((n_exp, LANES), F32)],
        compiler_params=_cparams(("arbitrary", "arbitrary")),
        name="router",
    )(x, mod, rhi, rlo, tri)


def _row_copy(src_ref, src_row, dst_ref, dst_row, sem):
    return pltpu.make_async_copy(src_ref.at[pl.ds(src_row, 1)], dst_ref.at[pl.ds(dst_row, 1)], sem)


def _scatter_kernel(pos_ref, x_ref, m_ref, init_ref, xs_ref, h_scr, sem, *, d):
    del init_ref
    tm = x_ref.shape[0]
    h_scr[...] = _ffn_input(x_ref[...], m_ref, d)

    def issue(t, carry):
        for k in range(2):
            _row_copy(h_scr, t, xs_ref, pos_ref[k, t], sem).start()
        return carry

    lax.fori_loop(0, tm, issue, 0, unroll=8)
    for _ in range(2):
        pltpu.make_async_copy(h_scr, xs_ref.at[pl.ds(0, tm)], sem).wait()


def _scatter_rows(pos, x, mod, mod_row, n_rows, *, tm):
    b, l, d = x.shape
    per_b = l // tm
    return pl.pallas_call(
        functools.partial(_scatter_kernel, d=d),
        grid=(b, per_b),
        in_specs=[pl.BlockSpec((2, tm), lambda bi, i: (0, bi * per_b + i), memory_space=pltpu.SMEM),
                  pl.BlockSpec((None, tm, d), lambda bi, i: (bi, i, 0)), _mod_spec(d, mod_row),
                  pl.BlockSpec(memory_space=pl.ANY)],
        out_specs=pl.BlockSpec(memory_space=pl.ANY),
        out_shape=jax.ShapeDtypeStruct((n_rows, d), F32),
        scratch_shapes=[pltpu.VMEM((tm, d), F32), pltpu.SemaphoreType.DMA(())],
        input_output_aliases={3: 0},
        compiler_params=_cparams(("arbitrary", "arbitrary")),
        name="scatter_rows",
    )(pos, x, mod, jnp.zeros((n_rows, d), F32))


def _expert_kernel(te_ref, nu_ref, x_ref, wg_ref, wu_ref, wd_ref, o_ref, h_scr):
    del te_ref
    i, j = pl.program_id(0), pl.program_id(1)

    @pl.when(j == 0)
    def _():
        o_ref[...] = jnp.zeros_like(o_ref)

    @pl.when(i < nu_ref[0])
    def _():
        @pl.when(j == 0)
        def _():
            h_scr[...] = x_ref[...].astype(BF16)

        o_ref[...] += _swiglu(h_scr[...], wg_ref, wu_ref, wd_ref)


def _expert_ffn(tile_expert, n_used, xs, wg, wu, wd, *, tm, tf):
    n_rows, w = xs.shape
    n_exp, d, dff = wg.shape
    tf = tf if dff % tf == 0 else dff
    live =lambda i, nu: jnp.minimum(i, nu[0] - 1)
    grid_spec = pltpu.PrefetchScalarGridSpec(
        num_scalar_prefetch=2,
        grid=(n_rows // tm, dff // tf),
        in_specs=[pl.BlockSpec((tm, w), lambda i, j, te, nu: (live(i, nu), 0)),
                  pl.BlockSpec((None, d, tf), lambda i, j, te, nu: (te[live(i, nu)], 0, jnp.where(i < nu[0], j, dff // tf - 1))),
                  pl.BlockSpec((None, d, tf), lambda i, j, te, nu: (te[live(i, nu)], 0, jnp.where(i < nu[0], j, dff // tf - 1))),
                  pl.BlockSpec((None, tf, d), lambda i, j, te, nu: (te[live(i, nu)], jnp.where(i < nu[0], j, dff // tf - 1), 0))],
        out_specs=pl.BlockSpec((tm, w), lambda i, j, te, nu: (i, 0)),
        scratch_shapes=[pltpu.VMEM((tm, d), BF16)],
    )
    return pl.pallas_call(
        _expert_kernel,
        grid_spec=grid_spec,
        out_shape=jax.ShapeDtypeStruct((n_rows, w), F32),
        compiler_params=_cparams(("arbitrary", "arbitrary")),
        name="expert_ffn",
    )(tile_expert, n_used, xs, wg, wu, wd)


def _combine_kernel(pos_ref, ys_ref, wts_ref, x_ref, m_ref, g_ref, b_ref, o_ref, buf, sem, *, d, alpha):
    tm = x_ref.shape[0]

    def issue(t, carry):
        for k in range(2):
            pltpu.make_async_copy(ys_ref.at[pl.ds(pos_ref[k, t], 1)], buf.at[k, pl.ds(t, 1)], sem).start()
        return carry

    lax.fori_loop(0, tm, issue, 0, unroll=8)
    for k in range(2):
        pltpu.make_async_copy(ys_ref.at[pl.ds(0, tm)], buf.at[k], sem).wait()
    w = wts_ref[...].T
    f = w[:, 0:1] * buf[0] + w[:, 1:2] * buf[1]
    o_ref[...] = _post_norm(x_ref[...], m_ref[:, 5 * d:6 * d] * f, g_ref[...], b_ref[...], alpha)


def _combine(pos, ys, wts, x, mod, mod_row, gain, bias, *, tm, alpha):
    b, l, d = x.shape
    per_b = l // tm
    flat = lambda bi, i: (0, bi * per_b + i)
    tok = pl.BlockSpec((None, tm, d), lambda bi, i: (bi, i, 0))
    return pl.pallas_call(
        functools.partial(_combine_kernel, d=d, alpha=alpha),
        grid=(b, per_b),
        in_specs=[pl.BlockSpec((2, tm), flat, memory_space=pltpu.SMEM),
                  pl.BlockSpec(memory_space=pl.ANY),
                  pl.BlockSpec((8, tm), flat), tok,
                  _mod_spec(d, mod_row),
                  _const_spec((1, d)), _const_spec((1, d))],
        out_specs=tok,
        out_shape=jax.ShapeDtypeStruct((b, l, d), F32),
        scratch_shapes=[pltpu.VMEM((2, tm, d), F32), pltpu.SemaphoreType.DMA(())],
        compiler_params=_cparams(("arbitrary", "arbitrary")),
        name="combine",
    )(pos, ys, wts, x, mod, gain, bias)


def _moe_ffn(x, mod, mod_row, router, wg, wu, wd, gain, bias, *, tm, tm_e, tf, alpha):
    b, l, d = x.shape
    n = b * l
    n_exp = router.shape[1]
    idx, wts, rank, counts = _route(x, mod, mod_row, router, tm=tm)
    cnt = counts[:, 0].astype(I32)
    tiles = (cnt + tm_e - 1) // tm_e
    tile_end = jnp.cumsum(tiles)
    offsets = (tile_end - tiles) * tm_e
    eid = jnp.arange(n_exp, dtype=I32)
    pos = rank + jnp.sum(jnp.where(idx[None] == eid[:, None, None], offsets[:, None, None], 0), axis=0)
    n_tiles = (2 * n) // tm_e + n_exp
    tile_ids = jnp.arange(n_tiles, dtype=I32)
    tile_expert = jnp.minimum(jnp.sum((tile_ids[:, None] >= tile_end[None, :]).astype(I32), axis=1), n_exp - 1)
    n_used = tile_end[-1:].astype(I32)
    xs = _scatter_rows(pos, x, mod, mod_row, n_tiles * tm_e, tm=tm)
    ys = _expert_ffn(tile_expert, n_used, xs, wg, wu, wd, tm=tm_e, tf=tf)
    return _combine(pos, ys, wts, x, mod, mod_row, gain, bias, tm=tm, alpha=alpha)


def _token_mixer_consts(q_norm, k_norm, sg_w, sg_b):
    qg = (jnp.tile(q_norm, N_Q_HEADS) * (HEAD_DIM ** -0.5 * math.log2(math.e))).reshape(1, ATTN_W)
    kg = jnp.tile(k_norm, N_KV_HEADS).reshape(1, KV_W)
    sgb = jnp.broadcast_to(sg_b[:, :, None], (N_GROUPS, CHUNK, GROUP_W))
    return qg, kg, sg_w.astype(BF16), sgb


def kernel(x, c, ctx, c_ctx, w_mod, b_mod, w_in, q_norm, k_norm, sg_w, sg_b, w_branch, w_out, ln1_g, ln1_b, ln2_g, ln2_b, ffn_w_gate, ffn_w_up, ffn_w_down, router, exp_w_gate, exp_w_up, exp_w_down):
    b, s, d = x.shape
    cl = ctx.shape[1]
    depth = w_in.shape[0]
    assert s % (GRID_W * FFT_L2) == 0 and s % 512 == 0 and cl % CHUNK == 0 and b + 1 <= MOD_ROWS
    alpha = (2 * depth) ** 0.25

    cond = jnp.zeros((MOD_ROWS, d), F32).at[:b].set(c).at[b].set(c_ctx)
    mod = _modulation(cond, w_mod, b_mod).reshape(depth, MOD_ROWS, 1, 6 * d)

    t = np.arange(s)
    lat_tables = _rope_tables(t // GRID_W, t % GRID_W)
    ctx_tables = _rope_tables(np.zeros(cl), np.zeros(cl))
    head = np.arange(ATTN_W) // HEAD_DIM
    gmat = _bf16_const(head[:, None] == head[None, :])
    cc, sc = _dft_cos_sin(GROUP_W)
    dftc = _bf16_const(np.concatenate([cc, sc], axis=1))
    row2 = lambda v: v.reshape(1, d)

    x_lat, x_ctx = x, ctx
    for l in range(depth):
        last = l == depth - 1
        m = mod[l]
        consts = _token_mixer_consts(q_norm[l], k_norm[l], sg_w[l], sg_b[l])
        w_in_l = w_in[l].astype(BF16)
        wb, wo = w_branch[l].astype(BF16), w_out[l].astype(BF16)
        proj = functools.partial(_in_projection, w_in=w_in_l, qg=consts[0], kg=consts[1], gmat=gmat,
                                 sgw=consts[2], sgb=consts[3], dftc=dftc)
        q, k, vt, ysg, ftcs, gate = proj(x_lat, m, None, tables=lat_tables, tm=512, kv_only=False)
        ctx_out = proj(x_ctx, m, b, tables=ctx_tables, tm=cl, kv_only=last)
        k_c, vt_c = ctx_out[:2] if last else ctx_out[1:3]
        ya = _attention(q, jnp.concatenate([k_c, k], axis=1), jnp.concatenate([vt_c, vt], axis=2), tq=512)
        yft = _fourier_latent(ftcs)
        x_lat = _merge(ya, ysg, yft, gate, x_lat, m, None, wb, wo, row2(ln1_g[l]), row2(ln1_b[l]), tm=512, alpha=alpha)
        if not last:
            q_c, _, _, ysg_c, ftcs_c, gate_c = ctx_out
            ya_c = _attention(q_c, k_c, vt_c, tq=cl)
            x_ctx = _merge(ya_c, ysg_c, _fourier_direct(ftcs_c), gate_c, x_ctx, m, b, wb, wo,
                           row2(ln1_g[l]), row2(ln1_b[l]), tm=cl, alpha=alpha)
        i = l // 2
        if l % 2 == 0:
            wg, wu, wd = ffn_w_gate[i].astype(BF16), ffn_w_up[i].astype(BF16), ffn_w_down[i].astype(BF16)
            ffn = functools.partial(_dense_ffn, wg=wg, wu=wu, wd=wd, gain=row2(ln2_g[l]), bias=row2(ln2_b[l]), alpha=alpha)
            x_lat = ffn(x_lat, m, None, tm=512)
            if not last:
                x_ctx = ffn(x_ctx, m, b, tm=cl)
        else:
            moe = functools.partial(_moe_ffn, router=router[i], wg=exp_w_gate[i].astype(BF16),
                                    wu=exp_w_up[i].astype(BF16), wd=exp_w_down[i].astype(BF16),
                                    gain=row2(ln2_g[l]), bias=row2(ln2_b[l]), tm_e=512, tf=1792, alpha=alpha)
            x_lat = moe(x_lat, m, None, tm=512)
            if not last:
                x_ctx = moe(x_ctx, m, b, tm=cl)
    return x_lat
```

```python
import functools
import math

import numpy as np
import jax
import jax.numpy as jnp
from jax import lax
from jax.experimental import pallas as pl
from jax.experimental.pallas import tpu as pltpu

F32 = jnp.float32
BF16 = jnp.bfloat16
I32 = jnp.int32

N_Q_HEADS = 8
N_KV_HEADS = 2
GQA_GROUP = N_Q_HEADS // N_KV_HEADS
HEAD_DIM = 64
ATTN_W = N_Q_HEADS * HEAD_DIM
KV_W = N_KV_HEADS * HEAD_DIM
GRID_W = 64
ROPE_THETA = 10000.0
CHUNK = 128
N_GROUPS = 4
GROUP_W = 128
BRANCH_W = N_GROUPS * GROUP_W
N_BRANCH = 3
OFF_Q = 0
OFF_K = OFF_Q + ATTN_W
OFF_V = OFF_K + KV_W
OFF_U = OFF_V + KV_W
OFF_SGV = OFF_U + BRANCH_W
OFF_FT = OFF_SGV + BRANCH_W
OFF_GATE = OFF_FT + BRANCH_W
LN_EPS = 1e-6
RMS_EPS = 1e-6
GELU_C = math.sqrt(2.0 / math.pi)

LANES = 128
V7X_VMEM_BYTES = 64 * 1024 * 1024
VMEM_LIMIT = V7X_VMEM_BYTES - 8 * 1024 * 1024
SUBLANES = 8
FFT_L2 = SUBLANES * SUBLANES
MOD_ROWS = 16
SCORE_BOUND = 100.0
SCORE_BOUND_MARGIN = 1.0 + 2.0 ** -6


def _cparams(sem):
    return pltpu.CompilerParams(dimension_semantics=sem, vmem_limit_bytes=VMEM_LIMIT)


def _const_spec(shape):
    nd = len(shape)
    return pl.BlockSpec(shape, lambda *_: (0,) * nd, pipeline_mode=pl.Buffered(1))


def _mod_spec(d, mod_row):
    if mod_row is None:
        return pl.BlockSpec((None, 1, 6 * d), lambda bi, i: (bi, 0, 0))
    return pl.BlockSpec((None, 1, 6 * d), lambda bi, i: (mod_row, 0, 0))


def _bf16_const(a):
    return jnp.asarray(a, F32).astype(BF16)


def _dot(a, b):
    return jnp.dot(a, b, preferred_element_type=F32)


def _dot_nt(a, b):
    return lax.dot_general(a, b, (((1,), (1,)), ((), ())), preferred_element_type=F32)


def _sigmoid(x):
    return 0.5 * jnp.tanh(0.5 * x) + 0.5


def _gelu_tanh(x):
    return 0.5 * x * (1.0 + jnp.tanh(GELU_C * (x + 0.044715 * (x * x * x))))


def _normalize(x):
    mu = jnp.mean(x, axis=-1, keepdims=True)
    xc = x - mu
    var = jnp.mean(xc * xc, axis=-1, keepdims=True)
    return xc * lax.rsqrt(var + LN_EPS)


def _split_bf16(x):
    hi = x.astype(BF16)
    lo = (x - hi.astype(F32)).astype(BF16)
    return hi, lo


def _mod_kernel(c_ref, w_ref, b_ref, o_ref):
    c = c_ref[...]
    s = c * _sigmoid(c)
    o_ref[...] = _dot(s.astype(BF16), w_ref[...].astype(BF16)) + b_ref[...]


def _modulation(cond, w_mod, b_mod):
    depth, d, d6 = w_mod.shape
    tn = 2 * d if d6 % (2 * d) == 0 else d
    return pl.pallas_call(
        _mod_kernel,
        grid=(depth, d6 // tn),
        in_specs=[
            pl.BlockSpec((MOD_ROWS, d), lambda l, j: (0, 0)),
            pl.BlockSpec((None, d, tn), lambda l, j: (l, 0, j)),
            pl.BlockSpec((None, 1, tn), lambda l, j: (l, 0, j)),
        ],
        out_specs=pl.BlockSpec((None, MOD_ROWS, tn), lambda l, j: (l, 0, j)),
        out_shape=jax.ShapeDtypeStruct((depth, MOD_ROWS, d6), F32),
        compiler_params=_cparams(("parallel", "parallel")),
        name="modulation",
    )(cond, w_mod, b_mod.reshape(depth, 1, d6))


def _rope_tables(pos_row, pos_col):
    half = HEAD_DIM // 2
    d2 = half // 2
    lane = np.arange(LANES) % HEAD_DIM
    inv = ROPE_THETA ** (-(lane % d2).astype(np.float64) / d2)
    pos = np.where((lane < half)[None, :], pos_row[:, None], pos_col[:, None]).astype(np.float64)
    ang = pos * inv[None, :]
    first = ((lane % half) < d2)[None, :]
    cos = np.cos(ang)
    sin = np.sin(ang)
    sin_a = np.where(first, -sin, 0.0)
    sin_b = np.where(first, 0.0, sin)
    return (jnp.asarray(cos, F32), jnp.asarray(sin_a, F32), jnp.asarray(sin_b, F32))


def _rope(x, cos, sin_a, sin_b):
    d2 = HEAD_DIM // 4
    return x * cos + pltpu.roll(x, LANES - d2, 1) * sin_a + pltpu.roll(x, d2, 1) * sin_b


def _head_rms(z, gmat, gain):
    hi, lo = _split_bf16(z * z)
    ss = _dot(hi, gmat) + _dot(lo, gmat)
    return z * lax.rsqrt(ss * (1.0 / HEAD_DIM) + RMS_EPS) * gain


def _inproj_kernel(x_ref, m_ref, w_ref, cos_ref, sa_ref, sb_ref, qg_ref, kg_ref, gm_ref, sgw_ref, sgb_ref,
                   dft_ref, *out_refs, d, kv_only):
    tm = x_ref.shape[0]
    h = (_normalize(x_ref[...]) * (1.0 + m_ref[:, d:2 * d]) + m_ref[:, 0:d]).astype(BF16)
    cos, sin_a, sin_b = cos_ref[...], sa_ref[...], sb_ref[...]

    def proj(a, b):
        return _dot(h, w_ref[:, a:b])

    if kv_only:
        k_ref, vt_ref = out_refs
    else:
        q_ref, k_ref, vt_ref, ysg_ref, ft_ref, gate_ref = out_refs

    def finish_q(z):
        qn = _head_rms(z, gm_ref[...], qg_ref[...])
        for s in range(ATTN_W // LANES):
            sl = slice(s * LANES, (s + 1) * LANES)
            q_ref[:, sl] = _rope(qn[:, sl], cos, sin_a, sin_b).astype(BF16)

    def finish_kv(z):
        kn = _head_rms(z[:, :KV_W], gm_ref[0:KV_W, 0:KV_W], kg_ref[...])
        k_ref[...] = _rope(kn, cos, sin_a, sin_b).astype(BF16)
        vt_ref[...] = z[:, KV_W:].T.astype(BF16)

    def finish_gating(z):
        u = _gelu_tanh(z[:, :BRANCH_W])
        v2 = _gelu_tanh(z[:, BRANCH_W:])
        for g in range(N_GROUPS):
            gl = slice(g * GROUP_W, (g + 1) * GROUP_W)
            vg = _normalize(v2[:, gl]).astype(BF16)
            for c in range(tm // CHUNK):
                rows = slice(c * CHUNK, (c + 1) * CHUNK)
                mixed = _dot(sgw_ref[g], vg[rows, :]) + sgb_ref[g]
                ysg_ref[rows, gl] = (u[rows, gl] * mixed).astype(BF16)

    def finish_fourier(z):
        zf = z.astype(BF16)
        for g in range(N_GROUPS):
            cs = _dot(zf[:, g * GROUP_W:(g + 1) * GROUP_W], dft_ref[...])
            ft_ref[:, g * GROUP_W:(g + 1) * GROUP_W] = cs[:, :GROUP_W]
            ft_ref[:, BRANCH_W + g * GROUP_W:BRANCH_W + (g + 1) * GROUP_W] = cs[:, GROUP_W:]

    def finish_gate(n):
        def store(z):
            gate_ref[:, n * d:(n + 1) * d] = _sigmoid(z).astype(BF16)
        return store

    if kv_only:
        finish_kv(proj(OFF_K, OFF_U))
        return
    stages = [((OFF_Q, OFF_K), finish_q), ((OFF_K, OFF_U), finish_kv), ((OFF_U, OFF_FT), finish_gating),
              ((OFF_FT, OFF_GATE), finish_fourier)]
    stages += [((OFF_GATE + n * d, OFF_GATE + (n + 1) * d), finish_gate(n)) for n in range(N_BRANCH)]
    z = proj(*stages[0][0])
    for i, (_, finish) in enumerate(stages):
        z_next = proj(*stages[i + 1][0]) if i + 1 < len(stages) else None
        finish(z)
        z = z_next


def _in_projection(x, mod, mod_row, w_in, tables, qg, kg, gmat, sgw, sgb, dftc, *, tm, kv_only):
    b, l, d = x.shape
    in_w = w_in.shape[1]
    cos, sin_a, sin_b = tables
    tok = lambda w: pl.BlockSpec((None, tm, w), lambda bi, i: (bi, i, 0))
    tab = pl.BlockSpec((tm, LANES), lambda bi, i: (i, 0))
    out_specs = [tok(KV_W), pl.BlockSpec((None, KV_W, tm), lambda bi, i: (bi, 0, i))]
    out_shape = [jax.ShapeDtypeStruct((b, l, KV_W), BF16), jax.ShapeDtypeStruct((b, KV_W, l), BF16)]
    if not kv_only:
        out_specs = [tok(ATTN_W)] + out_specs + [tok(BRANCH_W), tok(2 * BRANCH_W), tok(N_BRANCH * d)]
        out_shape = ([jax.ShapeDtypeStruct((b, l, ATTN_W), BF16)] + out_shape
                     + [jax.ShapeDtypeStruct((b, l, BRANCH_W), BF16),
                        jax.ShapeDtypeStruct((b, l, 2 * BRANCH_W), F32),
                        jax.ShapeDtypeStruct((b, l, N_BRANCH * d), BF16)])
    return pl.pallas_call(
        functools.partial(_inproj_kernel, d=d, kv_only=kv_only),
        grid=(b, l // tm),
        in_specs=[
            tok(d),
            _mod_spec(d, mod_row),
            _const_spec((d, in_w)),
            tab, tab, tab,
            _const_spec((1, ATTN_W)), _const_spec((1, KV_W)), _const_spec((ATTN_W, ATTN_W)),
            _const_spec((N_GROUPS, CHUNK, CHUNK)), _const_spec((N_GROUPS, CHUNK, GROUP_W)),
            _const_spec((GROUP_W, 2 * GROUP_W)),
        ],
        out_specs=out_specs,
        out_shape=out_shape,
        compiler_params=_cparams(("parallel", "parallel")),
        name="in_projection_kv" if kv_only else "in_projection",
    )(x, mod, w_in, cos, sin_a, sin_b, qg, kg, gmat, sgw, sgb, dftc)


def _attn_kernel(q_ref, k_ref, vt_ref, o_ref, p_scr, *, bounded):
    t = k_ref.shape[0]

    def q_transposed(hd):
        return q_ref[:, hd * HEAD_DIM:(hd + 1) * HEAD_DIM].astype(F32).T.astype(BF16)

    def weighted_values(hd, slot, l):
        kv = hd // GQA_GROUP
        o = _dot(vt_ref[kv * HEAD_DIM:(kv + 1) * HEAD_DIM, :], p_scr[slot]) * (1.0 / l)
        o_ref[:, hd * HEAD_DIM:(hd + 1) * HEAD_DIM] = o.T.astype(BF16)

    if bounded:
        def probabilities(hd, slot):
            kv = hd // GQA_GROUP
            qt = q_transposed(hd)
            lacc = None
            for j in range(t // LANES):
                rows = slice(j * LANES, (j + 1) * LANES)
                p = jnp.exp2(_dot(k_ref[rows, kv * HEAD_DIM:(kv + 1) * HEAD_DIM], qt))
                lacc = p if lacc is None else lacc + p
                p_scr[slot, rows, :] = p.astype(BF16)
            return jnp.sum(lacc, axis=0, keepdims=True)

        l_prev = None
        for hd in range(N_Q_HEADS):
            l_cur = probabilities(hd, hd % 2)
            if hd >= 1:
                weighted_values(hd - 1, (hd - 1) % 2, l_prev)
            l_prev = l_cur
        weighted_values(N_Q_HEADS - 1, (N_Q_HEADS - 1) % 2, l_prev)
        return

    rb = 256 if t % 256 == 0 else LANES
    nb = t // rb

    def scores(hd):
        kv = hd // GQA_GROUP
        return _dot(k_ref[:, kv * HEAD_DIM:(kv + 1) * HEAD_DIM], q_transposed(hd))

    def softmax(s, slot):
        macc = s[0:rb, :]
        for j in range(1, nb):
            macc = jnp.maximum(macc, s[j * rb:(j + 1) * rb, :])
        m = jnp.max(macc, axis=0, keepdims=True)
        lacc = None
        for j in range(nb):
            p = jnp.exp2(s[j * rb:(j + 1) * rb, :] - m)
            lacc = p if lacc is None else lacc + p
            p_scr[slot, j * rb:(j + 1) * rb, :] = p.astype(BF16)
        return jnp.sum(lacc, axis=0, keepdims=True)

    s_next = scores(0)
    l_prev = None
    for hd in range(N_Q_HEADS):
        s_cur = s_next
        if hd + 1 < N_Q_HEADS:
            s_next = scores(hd + 1)
        if hd >= 1:
            weighted_values(hd - 1, (hd - 1) % 2, l_prev)
        l_prev = softmax(s_cur, hd % 2)
    weighted_values(N_Q_HEADS - 1, (N_Q_HEADS - 1) % 2, l_prev)


def _attention_call(q, k, vt, *, tq, bounded):
    b, l, _ = q.shape
    t = k.shape[1]
    return pl.pallas_call(
        functools.partial(_attn_kernel, bounded=bounded),
        grid=(b, l // tq),
        in_specs=[
            pl.BlockSpec((None, tq, ATTN_W), lambda bi, i: (bi, i, 0)),
            pl.BlockSpec((None, t, KV_W), lambda bi, i: (bi, 0, 0)),
            pl.BlockSpec((None, KV_W, t), lambda bi, i: (bi, 0, 0)),
        ],
        out_specs=pl.BlockSpec((None, tq, ATTN_W), lambda bi, i: (bi, i, 0)),
        out_shape=jax.ShapeDtypeStruct((b, l, ATTN_W), BF16),
        scratch_shapes=[pltpu.VMEM((2, t, tq), BF16)],
        compiler_params=_cparams(("parallel", "parallel")),
        name="attention_bounded" if bounded else "attention",
    )(q, k, vt)


def _max_head_norm(z):
    zf = z.astype(F32).reshape(z.shape[:-1] + (z.shape[-1] // HEAD_DIM, HEAD_DIM))
    return jnp.sqrt(jnp.max(jnp.sum(zf * zf, axis=-1)))


def _attention(q, k, vt, *, tq):
    bound = _max_head_norm(q) * _max_head_norm(k) * SCORE_BOUND_MARGIN
    return lax.cond(bound <= SCORE_BOUND,
                    functools.partial(_attention_call, tq=tq, bounded=True),
                    functools.partial(_attention_call, tq=tq, bounded=False), q, k, vt)


def _dft_cos_sin(n):
    k = np.arange(n, dtype=np.float64)
    ang = 2.0 * np.pi * np.outer(k, k) / n
    return np.cos(ang), np.sin(ang)


def _fft_stage1_kernel(x_ref, w_ref, tc_ref, ts_ref, o_ref):
    l1 = x_ref.shape[0]
    half = l1 * SUBLANES
    x = x_ref[...].reshape(half, 2 * BRANCH_W)
    stacked = jnp.concatenate([x[:, :BRANCH_W], x[:, BRANCH_W:]], axis=0).astype(BF16)
    u = _dot(w_ref[...], stacked)
    tc = jnp.concatenate([tc_ref[...]] * N_GROUPS, axis=1)
    ts = jnp.concatenate([ts_ref[...]] * N_GROUPS, axis=1)
    ur, ui = u[:half], u[half:]
    o_ref[0] = (ur * tc + ui * ts).reshape(o_ref.shape[1:])
    o_ref[1] = (ui * tc - ur * ts).reshape(o_ref.shape[1:])


def _fft_stage2_kernel(z_ref, w_ref, o_ref, *, scale):
    z = z_ref[...].reshape(-1, BRANCH_W).astype(BF16)
    o_ref[...] = (_dot(w_ref[...], z) * scale).reshape(o_ref.shape)


def _fourier_latent(ftcs):
    b, s, _ = ftcs.shape
    sub, l2 = SUBLANES, FFT_L2
    l1 = s // l2
    assert l2 == sub * sub and l1 % sub == 0
    eye = np.eye(sub)
    c1, s1 = _dft_cos_sin(l1)
    c2, s2 = _dft_cos_sin(l2)
    w1 = _bf16_const(np.kron(np.block([[c1, -s1], [-s1, -c1]]), eye))
    cs2 = np.stack([c2, s2]).reshape(2, l2, sub, sub)
    w2 = _bf16_const(np.einsum("rkxy,lm->klxrmy", cs2, eye).reshape(l2 * sub, 2 * l2 * sub))
    bpos = np.arange(l2).reshape(sub, 1, sub)
    ang = 2.0 * np.pi * (np.arange(l1).reshape(1, l1, 1) * bpos) / s
    ang = np.repeat(ang.reshape(sub, l1 * sub, 1), GROUP_W, axis=2)
    tc, ts = jnp.asarray(np.cos(ang), F32), jnp.asarray(np.sin(ang), F32)
    x5 = ftcs.reshape(b, l1, sub, sub, 2 * BRANCH_W)
    z = pl.pallas_call(
        _fft_stage1_kernel,
        grid=(b, sub),
        in_specs=[
            pl.BlockSpec((None, l1, None, sub, 2 * BRANCH_W), lambda bi, i: (bi, 0, i, 0, 0)),
            _const_spec((2 * l1 * sub, 2 * l1 * sub)),
            pl.BlockSpec((None, l1 * sub, GROUP_W), lambda bi, i: (i, 0, 0)),
            pl.BlockSpec((None, l1 * sub, GROUP_W), lambda bi, i: (i, 0, 0)),
        ],
        out_specs=pl.BlockSpec((None, None, 2, l1 // sub, l2, BRANCH_W), lambda bi, i: (bi, i, 0, 0, 0, 0)),
        out_shape=jax.ShapeDtypeStruct((b, sub, 2, l1 // sub, l2, BRANCH_W), F32),
        compiler_params=_cparams(("parallel", "parallel")),
        name="fourier_stage1",
    )(x5, w1, tc, ts)
    y = pl.pallas_call(
        functools.partial(_fft_stage2_kernel, scale=1.0 / math.sqrt(s * GROUP_W)),
        grid=(b, l1 // sub),
        in_specs=[
            pl.BlockSpec((None, sub, 2, None, l2, BRANCH_W), lambda bi, i: (bi, 0, 0, i, 0, 0)),
            _const_spec((l2 * sub, 2 * l2 * sub)),
        ],
        out_specs=pl.BlockSpec((None, l2, None, sub, BRANCH_W), lambda bi, i: (bi, 0, i, 0, 0)),
        out_shape=jax.ShapeDtypeStruct((b, l2, l1 // sub, sub, BRANCH_W), F32),
        compiler_params=_cparams(("parallel", "parallel")),
        name="fourier_stage2",
    )(z, w2)
    return y.reshape(b, s, BRANCH_W)


def _fft_direct_kernel(x_ref, w_ref, o_ref, *, scale):
    x = x_ref[...]
    stacked = jnp.concatenate([x[:, :BRANCH_W], x[:, BRANCH_W:]], axis=0).astype(BF16)
    o_ref[...] = _dot(w_ref[...], stacked) * scale


def _fourier_direct(ftcs):
    b, l, _ = ftcs.shape
    c, s = _dft_cos_sin(l)
    w = _bf16_const(np.concatenate([c, -s], axis=1))
    return pl.pallas_call(
        functools.partial(_fft_direct_kernel, scale=1.0 / math.sqrt(l * GROUP_W)),
        grid=(b,),
        in_specs=[pl.BlockSpec((None, l, 2 * BRANCH_W), lambda bi: (bi, 0, 0)), _const_spec((l, 2 * l))],
        out_specs=pl.BlockSpec((None, l, BRANCH_W), lambda bi: (bi, 0, 0)),
        out_shape=jax.ShapeDtypeStruct((b, l, BRANCH_W), F32),
        compiler_params=_cparams(("parallel",)),
        name="fourier_direct",
    )(ftcs, w)


def _post_norm(x, y, gain, bias, alpha):
    return _normalize(alpha * x + y) * gain + bias


def _merge_kernel(ya_ref, ysg_ref, yft_ref, gate_ref, x_ref, m_ref, wb_ref, wo_ref, g_ref, b_ref, o_ref, *, d, alpha):
    acc = None
    for n, y_ref in enumerate((ya_ref, ysg_ref, yft_ref)):
        p = gate_ref[:, n * d:(n + 1) * d].astype(F32) * _dot(y_ref[...].astype(BF16), wb_ref[n])
        acc = p if acc is None else acc + p
    y = _dot(acc.astype(BF16), wo_ref[...])
    o_ref[...] = _post_norm(x_ref[...], m_ref[:, 2 * d:3 * d] * y, g_ref[...], b_ref[...], alpha)


def _merge(ya, ysg, yft, gate, x, mod, mod_row, wb, wo, gain, bias, *, tm, alpha):
    b, l, d = x.shape
    tok = lambda w: pl.BlockSpec((None, tm, w), lambda bi, i: (bi, i, 0))
    return pl.pallas_call(
        functools.partial(_merge_kernel, d=d, alpha=alpha),
        grid=(b, l // tm),
        in_specs=[tok(BRANCH_W), tok(BRANCH_W), tok(BRANCH_W), tok(N_BRANCH * d), tok(d),
                  _mod_spec(d, mod_row),
                  _const_spec((N_BRANCH, BRANCH_W, d)), _const_spec((d, d)),
                  _const_spec((1, d)), _const_spec((1, d))],
        out_specs=tok(d),
        out_shape=jax.ShapeDtypeStruct((b, l, d), F32),
        compiler_params=_cparams(("parallel", "parallel")),
        name="merge",
    )(ya, ysg, yft, gate, x, mod, wb, wo, gain, bias)


def _swiglu(h, wg_ref, wu_ref, wd_ref):
    gate = _dot(h, wg_ref[...])
    act = (gate * _sigmoid(gate) * _dot(h, wu_ref[...])).astype(BF16)
    return _dot(act, wd_ref[...])


def _ffn_kernel(x_ref, m_ref, wg_ref, wu_ref, wd_ref, g_ref, b_ref, o_ref, *, d, alpha):
    x = x_ref[...]
    h = _ffn_input(x, m_ref, d).astype(BF16)
    f = _swiglu(h, wg_ref, wu_ref, wd_ref)
    o_ref[...] = _post_norm(x, m_ref[:, 5 * d:6 * d] * f, g_ref[...], b_ref[...], alpha)


def _dense_ffn(x, mod, mod_row, wg, wu, wd, gain, bias, *, tm, alpha):
    b, l, d = x.shape
    dff = wg.shape[1]
    tok = pl.BlockSpec((None, tm, d), lambda bi, i: (bi, i, 0))
    return pl.pallas_call(
        functools.partial(_ffn_kernel, d=d, alpha=alpha),
        grid=(b, l // tm),
        in_specs=[tok, _mod_spec(d, mod_row),
                  _const_spec((d, dff)), _const_spec((d, dff)), _const_spec((dff, d)),
                  _const_spec((1, d)), _const_spec((1, d))],
        out_specs=tok,
        out_shape=jax.ShapeDtypeStruct((b, l, d), F32),
        compiler_params=_cparams(("parallel", "parallel")),
        name="dense_ffn",
    )(x, mod, wg, wu, wd, gain, bias)


def _ffn_input(x, m_ref, d):
    return _normalize(x) * (1.0 + m_ref[:, 4 * d:5 * d]) + m_ref[:, 3 * d:4 * d]


def _router_kernel(x_ref, m_ref, rhi_ref, rlo_ref, tri_ref, idx_ref, aux_ref, cnt_ref, *, d, n_exp):
    tm = x_ref.shape[0]
    hi, lo = _split_bf16(_ffn_input(x_ref[...], m_ref, d))
    logits = _dot_nt(rhi_ref[...], hi) + _dot_nt(rhi_ref[...], lo) + _dot_nt(rlo_ref[...], hi)
    eid = lax.broadcasted_iota(I32, (n_exp, tm), 0).astype(F32)
    v1 = jnp.max(logits, axis=0, keepdims=True)
    i1 = jnp.min(jnp.where(logits == v1, eid, float(n_exp)), axis=0, keepdims=True)
    sel1 = eid == i1
    rest = jnp.where(sel1, -jnp.inf, logits)
    v2 = jnp.max(rest, axis=0, keepdims=True)
    i2 = jnp.min(jnp.where(rest == v2, eid, float(n_exp)), axis=0, keepdims=True)
    sel2 = eid == i2
    w1 = 1.0 / (1.0 + jnp.exp(v2 - v1))
    onehot = jnp.where(sel1 | sel2, 1.0, 0.0)
    before = _dot(onehot.astype(BF16), tri_ref[...])
    idx_ref[0:1, :] = i1.astype(I32)
    idx_ref[1:2, :] = i2.astype(I32)
    aux_ref[...] = jnp.zeros_like(aux_ref)
    aux_ref[0:1, :] = w1
    aux_ref[1:2, :] = 1.0 - w1
    aux_ref[2:3, :] = jnp.sum(jnp.where(sel1, before, 0.0), axis=0, keepdims=True)
    aux_ref[3:4, :] = jnp.sum(jnp.where(sel2, before, 0.0), axis=0, keepdims=True)
    cnt_ref[...] = jnp.broadcast_to(jnp.sum(onehot, axis=1, keepdims=True), cnt_ref.shape)


def _route(x, mod, mod_row, router, *, tm):
    b, l, d = x.shape
    n_exp = router.shape[1]
    n = b * l
    rt = router.T
    rhi = rt.astype(BF16)
    rlo = (rt - rhi.astype(F32)).astype(BF16)
    tri = _bf16_const(np.triu(np.ones((tm, tm)), 1))
    per_b = l // tm
    flat = lambda bi, i: (0, bi * per_b + i)
    return pl.pallas_call(
        functools.partial(_router_kernel, d=d, n_exp=n_exp),
        grid=(b, per_b),
        in_specs=[pl.BlockSpec((None, tm, d), lambda bi, i: (bi, i, 0)),
                  _mod_spec(d, mod_row),
                  _const_spec((n_exp, d)), _const_spec((n_exp, d)), _const_spec((tm, tm))],
        out_specs=[pl.BlockSpec((2, tm), flat), pl.BlockSpec((SUBLANES, tm), flat),
                   pl.BlockSpec((None, n_exp, LANES), lambda bi, i: (bi * per_b + i, 0, 0))],
        out_shape=[jax.ShapeDtypeStruct((2, n), I32), jax.ShapeDtypeStruct((SUBLANES, n), F32),
                   jax.ShapeDtypeStruct((n // tm, n_exp, LANES), F32)],
        compiler_params=_cparams(("parallel", "parallel")),
        name="router",
    )(x, mod, rhi, rlo, tri)


def _local_slots(idx_ref, aux_ref, start_ref):
    idx = idx_ref[...]
    base = jnp.zeros(idx.shape, F32)
    for e in range(start_ref.shape[1]):
        base = jnp.where(idx == e, start_ref[0, e].astype(F32), base)
    return base + aux_ref[2:4, :]


def _chunk_copy(src_ref, src_row, dst_ref, dst_row, sem):
    return pltpu.make_async_copy(src_ref.at[pl.ds(pl.multiple_of(src_row, SUBLANES), SUBLANES)],
                                 dst_ref.at[pl.ds(pl.multiple_of(dst_row, SUBLANES), SUBLANES)], sem)


def _scatter_kernel(rows_ref, start_ref, idx_ref, aux_ref, x_ref, m_ref, init_ref, xs_ref, buf, sem, *, d):
    del init_ref
    n_rows = buf.shape[0]
    h = _ffn_input(x_ref[...], m_ref, d).astype(BF16)
    slot = _local_slots(idx_ref, aux_ref, start_ref)
    row = lax.broadcasted_iota(I32, (n_rows, h.shape[0]), 0).astype(F32)
    place = jnp.where((row == slot[0:1, :]) | (row == slot[1:2, :]), 1.0, 0.0).astype(BF16)
    buf[...] = _dot(place, h)

    def issue(c, carry):
        _chunk_copy(buf, c * SUBLANES, xs_ref, rows_ref[0, c], sem).start()
        return carry

    lax.fori_loop(0, n_rows // SUBLANES, issue, 0, unroll=8)
    pltpu.make_async_copy(buf, xs_ref.at[pl.ds(0, n_rows)], sem).wait()


def _scatter_rows(chunk_rows, seg_start, idx, aux, x, mod, mod_row, n_rows, *, tm):
    b, l, d = x.shape
    per_b = l // tm
    n_chunks, n_exp = chunk_rows.shape[-1], seg_start.shape[-1]
    tile = lambda bi, i: (bi * per_b + i, 0, 0)
    flat = lambda bi, i: (0, bi * per_b + i)
    return pl.pallas_call(
        functools.partial(_scatter_kernel, d=d),
        grid=(b, per_b),
        in_specs=[pl.BlockSpec((None, 1, n_chunks), tile, memory_space=pltpu.SMEM),
                  pl.BlockSpec((None, 1, n_exp), tile, memory_space=pltpu.SMEM),
                  pl.BlockSpec((2, tm), flat), pl.BlockSpec((SUBLANES, tm), flat),
                  pl.BlockSpec((None, tm, d), lambda bi, i: (bi, i, 0)), _mod_spec(d, mod_row),
                  pl.BlockSpec(memory_space=pl.ANY)],
        out_specs=pl.BlockSpec(memory_space=pl.ANY),
        out_shape=jax.ShapeDtypeStruct((n_rows, d), F32),
        scratch_shapes=[pltpu.VMEM((n_chunks * SUBLANES, d), F32), pltpu.SemaphoreType.DMA(())],
        input_output_aliases={6: 0},
        compiler_params=_cparams(("arbitrary", "arbitrary")),
        name="scatter_rows",
    )(chunk_rows, seg_start, idx, aux, x, mod, jnp.zeros((n_rows, d), F32))


def _expert_kernel(te_ref, nu_ref, x_ref, wg_ref, wu_ref, wd_ref, o_ref, h_scr):
    del te_ref
    i, j = pl.program_id(0), pl.program_id(1)

    @pl.when(j == 0)
    def _():
        o_ref[...] = jnp.zeros_like(o_ref)

    @pl.when(i < nu_ref[0])
    def _():
        @pl.when(j == 0)
        def _():
            h_scr[...] = x_ref[...].astype(BF16)

        o_ref[...] += _swiglu(h_scr[...], wg_ref, wu_ref, wd_ref)


def _expert_ffn(tile_expert, n_used, xs, wg, wu, wd, *, tm, tf):
    n_rows, w = xs.shape
    n_exp, d, dff = wg.shape
    tf = tf if dff % tf == 0 else dff
    live =lambda i, nu: jnp.minimum(i, nu[0] - 1)
    grid_spec = pltpu.PrefetchScalarGridSpec(
        num_scalar_prefetch=2,
        grid=(n_rows // tm, dff // tf),
        in_specs=[pl.BlockSpec((tm, w), lambda i, j, te, nu: (live(i, nu), 0)),
                  pl.BlockSpec((None, d, tf), lambda i, j, te, nu: (te[live(i, nu)], 0, jnp.where(i < nu[0], j, dff // tf - 1))),
                  pl.BlockSpec((None, d, tf), lambda i, j, te, nu: (te[live(i, nu)], 0, jnp.where(i < nu[0], j, dff // tf - 1))),
                  pl.BlockSpec((None, tf, d), lambda i, j, te, nu: (te[live(i, nu)], jnp.where(i < nu[0], j, dff // tf - 1), 0))],
        out_specs=pl.BlockSpec((tm, w), lambda i, j, te, nu: (i, 0)),
        scratch_shapes=[pltpu.VMEM((tm, d), BF16)],
    )
    return pl.pallas_call(
        _expert_kernel,
        grid_spec=grid_spec,
        out_shape=jax.ShapeDtypeStruct((n_rows, w), F32),
        compiler_params=_cparams(("arbitrary", "arbitrary")),
        name="expert_ffn",
    )(tile_expert, n_used, xs, wg, wu, wd)


def _combine_kernel(rows_ref, start_ref, idx_ref, aux_ref, ys_ref, x_ref, m_ref, g_ref, b_ref, o_ref, buf, sem,
                    *, d, alpha):
    n_rows = buf.shape[0]

    def issue(c, carry):
        _chunk_copy(ys_ref, rows_ref[0, c], buf, c * SUBLANES, sem).start()
        return carry

    lax.fori_loop(0, n_rows // SUBLANES, issue, 0, unroll=8)
    slot = _local_slots(idx_ref, aux_ref, start_ref)
    cols = jnp.concatenate([slot, aux_ref[0:2, :], jnp.zeros((SUBLANES - 4, slot.shape[1]), F32)], axis=0).T
    lane = lax.broadcasted_iota(I32, (cols.shape[0], n_rows), 1).astype(F32)
    mix = (jnp.where(lane == cols[:, 0:1], cols[:, 2:3], 0.0)
           + jnp.where(lane == cols[:, 1:2], cols[:, 3:4], 0.0)).astype(BF16)
    pltpu.make_async_copy(ys_ref.at[pl.ds(0, n_rows)], buf, sem).wait()
    f = _dot(mix, buf[...].astype(BF16))
    o_ref[...] = _post_norm(x_ref[...], m_ref[:, 5 * d:6 * d] * f, g_ref[...], b_ref[...], alpha)


def _combine(chunk_rows, seg_start, idx, aux, ys, x, mod, mod_row, gain, bias, *, tm, alpha):
    b, l, d = x.shape
    per_b = l // tm
    n_chunks, n_exp = chunk_rows.shape[-1], seg_start.shape[-1]
    tile = lambda bi, i: (bi * per_b + i, 0, 0)
    flat = lambda bi, i: (0, bi * per_b + i)
    tok = pl.BlockSpec((None, tm, d), lambda bi, i: (bi, i, 0))
    return pl.pallas_call(
        functools.partial(_combine_kernel, d=d, alpha=alpha),
        grid=(b, per_b),
        in_specs=[pl.BlockSpec((None, 1, n_chunks), tile, memory_space=pltpu.SMEM),
                  pl.BlockSpec((None, 1, n_exp), tile, memory_space=pltpu.SMEM),
                  pl.BlockSpec((2, tm), flat), pl.BlockSpec((SUBLANES, tm), flat),
                  pl.BlockSpec(memory_space=pl.ANY), tok,
                  _mod_spec(d, mod_row),
                  _const_spec((1, d)), _const_spec((1, d))],
        out_specs=tok,
        out_shape=jax.ShapeDtypeStruct((b, l, d), F32),
        scratch_shapes=[pltpu.VMEM((n_chunks * SUBLANES, d), F32), pltpu.SemaphoreType.DMA(())],
        compiler_params=_cparams(("arbitrary", "arbitrary")),
        name="combine",
    )(chunk_rows, seg_start, idx, aux, ys, x, mod, gain, bias)


def _moe_ffn(x, mod, mod_row, router, wg, wu, wd, gain, bias, *, tm, tm_e, tf, alpha):
    b, l, d = x.shape
    n = b * l
    n_exp = router.shape[1]
    sub, nt = SUBLANES, n // tm
    idx, aux, counts = _route(x, mod, mod_row, router, tm=tm)
    cnt = counts[:, :, 0].astype(I32)
    seg = (cnt + sub - 1) // sub * sub
    seg_start = jnp.cumsum(seg, axis=1) - seg
    tiles = (jnp.sum(seg, axis=0) + tm_e - 1) // tm_e
    tile_end = jnp.cumsum(tiles)
    base = ((tile_end - tiles) * tm_e)[None, :] + jnp.cumsum(seg, axis=0) - seg
    n_chunks = (2 * tm) // sub + n_exp
    n_main = -(-(2 * n + nt * n_exp * (sub - 1)) // tm_e) + n_exp
    n_tiles = n_main + -(-(nt * n_exp * sub) // tm_e)
    c0 = jnp.arange(n_chunks, dtype=I32) * sub
    inside = (c0[None, :, None] >= seg_start[:, None, :]) & (c0[None, :, None] < (seg_start + seg)[:, None, :])
    target = jnp.sum(jnp.where(inside, (base - seg_start)[:, None, :] + c0[None, :, None], 0), axis=2)
    spare = (c0[None, :] - jnp.sum(seg, axis=1)[:, None]) // sub
    dump = n_main * tm_e + (jnp.arange(nt, dtype=I32)[:, None] * n_exp + spare) * sub
    chunk_rows = jnp.where(jnp.any(inside, axis=2), target, dump).reshape(nt, 1, n_chunks)
    seg_start = seg_start.reshape(nt, 1, n_exp)
    tile_ids = jnp.arange(n_tiles, dtype=I32)
    tile_expert = jnp.minimum(jnp.sum((tile_ids[:, None] >= tile_end[None, :]).astype(I32), axis=1), n_exp - 1)
    n_used = tile_end[-1:].astype(I32)
    xs = _scatter_rows(chunk_rows, seg_start, idx, aux, x, mod, mod_row, n_tiles * tm_e, tm=tm)
    ys = _expert_ffn(tile_expert, n_used, xs, wg, wu, wd, tm=tm_e, tf=tf)
    return _combine(chunk_rows, seg_start, idx, aux, ys, x, mod, mod_row, gain, bias, tm=tm, alpha=alpha)


def _token_mixer_consts(q_norm, k_norm, sg_w, sg_b):
    qg = (jnp.tile(q_norm, N_Q_HEADS) * (HEAD_DIM ** -0.5 * math.log2(math.e))).reshape(1, ATTN_W)
    kg = jnp.tile(k_norm, N_KV_HEADS).reshape(1, KV_W)
    sgb = jnp.broadcast_to(sg_b[:, :, None], (N_GROUPS, CHUNK, GROUP_W))
    return qg, kg, sg_w.astype(BF16), sgb


def kernel(x, c, ctx, c_ctx, w_mod, b_mod, w_in, q_norm, k_norm, sg_w, sg_b, w_branch, w_out, ln1_g, ln1_b, ln2_g, ln2_b, ffn_w_gate, ffn_w_up, ffn_w_down, router, exp_w_gate, exp_w_up, exp_w_down):
    b, s, d = x.shape
    cl = ctx.shape[1]
    depth = w_in.shape[0]
    assert s % (GRID_W * FFT_L2) == 0 and s % 512 == 0 and cl % CHUNK == 0 and b + 1 <= MOD_ROWS
    alpha = (2 * depth) ** 0.25

    cond = jnp.zeros((MOD_ROWS, d), F32).at[:b].set(c).at[b].set(c_ctx)
    mod = _modulation(cond, w_mod, b_mod).reshape(depth, MOD_ROWS, 1, 6 * d)

    t = np.arange(s)
    lat_tables = _rope_tables(t // GRID_W, t % GRID_W)
    ctx_tables = _rope_tables(np.zeros(cl), np.zeros(cl))
    head = np.arange(ATTN_W) // HEAD_DIM
    gmat = _bf16_const(head[:, None] == head[None, :])
    cc, sc = _dft_cos_sin(GROUP_W)
    dftc = _bf16_const(np.concatenate([cc, sc], axis=1))
    row2 = lambda v: v.reshape(1, d)

    x_lat, x_ctx = x, ctx
    for l in range(depth):
        last = l == depth - 1
        m = mod[l]
        consts = _token_mixer_consts(q_norm[l], k_norm[l], sg_w[l], sg_b[l])
        w_in_l = w_in[l].astype(BF16)
        wb, wo = w_branch[l].astype(BF16), w_out[l].astype(BF16)
        proj = functools.partial(_in_projection, w_in=w_in_l, qg=consts[0], kg=consts[1], gmat=gmat,
                                 sgw=consts[2], sgb=consts[3], dftc=dftc)
        q, k, vt, ysg, ftcs, gate = proj(x_lat, m, None, tables=lat_tables, tm=512, kv_only=False)
        ctx_out = proj(x_ctx, m, b, tables=ctx_tables, tm=cl, kv_only=last)
        k_c, vt_c = ctx_out[:2] if last else ctx_out[1:3]
        ya = _attention(q, jnp.concatenate([k_c, k], axis=1), jnp.concatenate([vt_c, vt], axis=2), tq=512)
        yft = _fourier_latent(ftcs)
        x_lat = _merge(ya, ysg, yft, gate, x_lat, m, None, wb, wo, row2(ln1_g[l]), row2(ln1_b[l]), tm=512, alpha=alpha)
        if not last:
            q_c, _, _, ysg_c, ftcs_c, gate_c = ctx_out
            ya_c = _attention(q_c, k_c, vt_c, tq=cl)
            x_ctx = _merge(ya_c, ysg_c, _fourier_direct(ftcs_c), gate_c, x_ctx, m, b, wb, wo,
                           row2(ln1_g[l]), row2(ln1_b[l]), tm=cl, alpha=alpha)
        i = l // 2
        if l % 2 == 0:
            wg, wu, wd = ffn_w_gate[i].astype(BF16), ffn_w_up[i].astype(BF16), ffn_w_down[i].astype(BF16)
            ffn = functools.partial(_dense_ffn, wg=wg, wu=wu, wd=wd, gain=row2(ln2_g[l]), bias=row2(ln2_b[l]), alpha=alpha)
            x_lat = ffn(x_lat, m, None, tm=512)
            if not last:
                x_ctx = ffn(x_ctx, m, b, tm=cl)
        else:
            moe = functools.partial(_moe_ffn, router=router[i], wg=exp_w_gate[i].astype(BF16),
                                    wu=exp_w_up[i].astype(BF16), wd=exp_w_down[i].astype(BF16),
                                    gain=row2(ln2_g[l]), bias=row2(ln2_b[l]), tm_e=512, tf=1792, alpha=alpha)
            x_lat = moe(x_lat, m, None, tm=512)
            if not last:
                x_ctx = moe(x_ctx, m, b, tm=cl)
    return x_lat
```

```python
import functools
import math

import numpy as np
import jax
import jax.numpy as jnp
from jax import lax
from jax.experimental import pallas as pl
from jax.experimental.pallas import tpu as pltpu

F32 = jnp.float32
BF16 = jnp.bfloat16
I32 = jnp.int32

N_Q_HEADS = 8
N_KV_HEADS = 2
GQA_GROUP = N_Q_HEADS // N_KV_HEADS
HEAD_DIM = 64
ATTN_W = N_Q_HEADS * HEAD_DIM
KV_W = N_KV_HEADS * HEAD_DIM
GRID_W = 64
ROPE_THETA = 10000.0
CHUNK = 128
N_GROUPS = 4
GROUP_W = 128
BRANCH_W = N_GROUPS * GROUP_W
N_BRANCH = 3
OFF_Q = 0
OFF_K = OFF_Q + ATTN_W
OFF_V = OFF_K + KV_W
OFF_U = OFF_V + KV_W
OFF_SGV = OFF_U + BRANCH_W
OFF_FT = OFF_SGV + BRANCH_W
OFF_GATE = OFF_FT + BRANCH_W
LN_EPS = 1e-6
RMS_EPS = 1e-6
GELU_C = math.sqrt(2.0 / math.pi)

LANES = 128
V7X_VMEM_BYTES = 64 * 1024 * 1024
VMEM_LIMIT = V7X_VMEM_BYTES - 8 * 1024 * 1024
SUBLANES = 8
FFT_L2 = SUBLANES * SUBLANES
MOD_ROWS = 16
SCORE_BOUND = 100.0
SCORE_BOUND_MARGIN = 1.0 + 2.0 ** -6


TOKEN_TILE = 512
EXPERT_FF_TILE = 1792


def _token_tile(length):
    return TOKEN_TILE if length % TOKEN_TILE == 0 else length


def _cparams(sem, fuse_inputs=None):
    fusion = None if fuse_inputs is None else [i in fuse_inputs[1] for i in range(fuse_inputs[0])]
    return pltpu.CompilerParams(dimension_semantics=sem, vmem_limit_bytes=VMEM_LIMIT, allow_input_fusion=fusion)


def _const_spec(shape):
    nd = len(shape)
    return pl.BlockSpec(shape, lambda *_: (0,) * nd, pipeline_mode=pl.Buffered(1))


def _mod_spec(d, mod_row):
    if mod_row is None:
        return pl.BlockSpec((None, 1, 6 * d), lambda bi, i: (bi, 0, 0))
    return pl.BlockSpec((None, 1, 6 * d), lambda bi, i: (mod_row, 0, 0))


def _bf16_const(a):
    return jnp.asarray(a, F32).astype(BF16)


def _dot(a, b):
    return jnp.dot(a, b, preferred_element_type=F32)


def _dot_nt(a, b):
    return lax.dot_general(a, b, (((1,), (1,)), ((), ())), preferred_element_type=F32)


def _sigmoid(x):
    return 0.5 * jnp.tanh(0.5 * x) + 0.5


def _gelu_tanh(x):
    return 0.5 * x * (1.0 + jnp.tanh(GELU_C * (x + 0.044715 * (x * x * x))))


def _normalize(x):
    mu = jnp.mean(x, axis=-1, keepdims=True)
    xc = x - mu
    var = jnp.mean(xc * xc, axis=-1, keepdims=True)
    return xc * lax.rsqrt(var + LN_EPS)


def _split_bf16(x):
    hi = x.astype(BF16)
    lo = (x - hi.astype(F32)).astype(BF16)
    return hi, lo


def _mod_kernel(c_ref, w_ref, b_ref, o_ref):
    c = c_ref[...]
    s = c * _sigmoid(c)
    o_ref[...] = _dot(s.astype(BF16), w_ref[...].astype(BF16)) + b_ref[...]


def _modulation(cond, w_mod, b_mod):
    depth, d, d6 = w_mod.shape
    tn = 2 * d if d6 % (2 * d) == 0 else d
    return pl.pallas_call(
        _mod_kernel,
        grid=(depth, d6 // tn),
        in_specs=[
            pl.BlockSpec((MOD_ROWS, d), lambda l, j: (0, 0)),
            pl.BlockSpec((None, d, tn), lambda l, j: (l, 0, j)),
            pl.BlockSpec((None, 1, tn), lambda l, j: (l, 0, j)),
        ],
        out_specs=pl.BlockSpec((None, MOD_ROWS, tn), lambda l, j: (l, 0, j)),
        out_shape=jax.ShapeDtypeStruct((depth, MOD_ROWS, d6), F32),
        compiler_params=_cparams(("parallel", "parallel")),
        name="modulation",
    )(cond, w_mod, b_mod.reshape(depth, 1, d6))


def _rope_tables(pos_row, pos_col):
    half = HEAD_DIM // 2
    d2 = half // 2
    lane = np.arange(LANES) % HEAD_DIM
    inv = ROPE_THETA ** (-(lane % d2).astype(np.float64) / d2)
    pos = np.where((lane < half)[None, :], pos_row[:, None], pos_col[:, None]).astype(np.float64)
    ang = pos * inv[None, :]
    first = ((lane % half) < d2)[None, :]
    cos = np.cos(ang)
    sin = np.sin(ang)
    sin_a = np.where(first, -sin, 0.0)
    sin_b = np.where(first, 0.0, sin)
    return (jnp.asarray(cos, F32), jnp.asarray(sin_a, F32), jnp.asarray(sin_b, F32))


def _rope(x, cos, sin_a, sin_b):
    d2 = HEAD_DIM // 4
    return x * cos + pltpu.roll(x, LANES - d2, 1) * sin_a + pltpu.roll(x, d2, 1) * sin_b


def _head_rms(z, gmat, gain):
    hi, lo = _split_bf16(z * z)
    ss = _dot(hi, gmat) + _dot(lo, gmat)
    return z * lax.rsqrt(ss * (1.0 / HEAD_DIM) + RMS_EPS) * gain


def _inproj_kernel(x_ref, m_ref, w_ref, cos_ref, sa_ref, sb_ref, qg_ref, kg_ref, gm_ref, sgw_ref, sgb_ref,
                   dft_ref, *out_refs, d, kv_only):
    tm = x_ref.shape[0]
    h = (_normalize(x_ref[...]) * (1.0 + m_ref[:, d:2 * d]) + m_ref[:, 0:d]).astype(BF16)
    cos, sin_a, sin_b = cos_ref[...], sa_ref[...], sb_ref[...]

    def proj(a, b):
        return _dot(h, w_ref[:, a:b])

    if kv_only:
        k_ref, vt_ref = out_refs
    else:
        q_ref, k_ref, vt_ref, ysg_ref, ft_ref, gate_ref = out_refs

    def finish_q(z):
        qn = _head_rms(z, gm_ref[...], qg_ref[...])
        for s in range(ATTN_W // LANES):
            sl = slice(s * LANES, (s + 1) * LANES)
            q_ref[:, sl] = _rope(qn[:, sl], cos, sin_a, sin_b).astype(BF16)

    def finish_kv(z):
        kn = _head_rms(z[:, :KV_W], gm_ref[0:KV_W, 0:KV_W], kg_ref[...])
        k_ref[...] = _rope(kn, cos, sin_a, sin_b).astype(BF16)
        vt_ref[...] = z[:, KV_W:].T.astype(BF16)

    def finish_gating(z):
        u = _gelu_tanh(z[:, :BRANCH_W])
        v2 = _gelu_tanh(z[:, BRANCH_W:])
        for g in range(N_GROUPS):
            gl = slice(g * GROUP_W, (g + 1) * GROUP_W)
            vg = _normalize(v2[:, gl]).astype(BF16)
            for c in range(tm // CHUNK):
                rows = slice(c * CHUNK, (c + 1) * CHUNK)
                mixed = _dot(sgw_ref[g], vg[rows, :]) + sgb_ref[g]
                ysg_ref[rows, gl] = (u[rows, gl] * mixed).astype(BF16)

    def finish_fourier(z):
        zf = z.astype(BF16)
        for g in range(N_GROUPS):
            cs = _dot(zf[:, g * GROUP_W:(g + 1) * GROUP_W], dft_ref[...])
            ft_ref[:, g * GROUP_W:(g + 1) * GROUP_W] = cs[:, :GROUP_W]
            ft_ref[:, BRANCH_W + g * GROUP_W:BRANCH_W + (g + 1) * GROUP_W] = cs[:, GROUP_W:]

    def finish_gate(n):
        def store(z):
            gate_ref[:, n * d:(n + 1) * d] = _sigmoid(z).astype(BF16)
        return store

    if kv_only:
        finish_kv(proj(OFF_K, OFF_U))
        return
    stages = [((OFF_Q, OFF_K), finish_q), ((OFF_K, OFF_U), finish_kv), ((OFF_U, OFF_FT), finish_gating),
              ((OFF_FT, OFF_GATE), finish_fourier)]
    stages += [((OFF_GATE + n * d, OFF_GATE + (n + 1) * d), finish_gate(n)) for n in range(N_BRANCH)]
    z = proj(*stages[0][0])
    for i, (_, finish) in enumerate(stages):
        z_next = proj(*stages[i + 1][0]) if i + 1 < len(stages) else None
        finish(z)
        z = z_next


def _in_projection(x, mod, mod_row, w_in, tables, qg, kg, gmat, sgw, sgb, dftc, *, tm, kv_only):
    b, l, d = x.shape
    in_w = w_in.shape[1]
    cos, sin_a, sin_b = tables
    tok = lambda w: pl.BlockSpec((None, tm, w), lambda bi, i: (bi, i, 0))
    tab = pl.BlockSpec((tm, LANES), lambda bi, i: (i, 0))
    out_specs = [tok(KV_W), pl.BlockSpec((None, KV_W, tm), lambda bi, i: (bi, 0, i))]
    out_shape = [jax.ShapeDtypeStruct((b, l, KV_W), BF16), jax.ShapeDtypeStruct((b, KV_W, l), BF16)]
    if not kv_only:
        out_specs = [tok(ATTN_W)] + out_specs + [tok(BRANCH_W), tok(2 * BRANCH_W), tok(N_BRANCH * d)]
        out_shape = ([jax.ShapeDtypeStruct((b, l, ATTN_W), BF16)] + out_shape
                     + [jax.ShapeDtypeStruct((b, l, BRANCH_W), BF16),
                        jax.ShapeDtypeStruct((b, l, 2 * BRANCH_W), F32),
                        jax.ShapeDtypeStruct((b, l, N_BRANCH * d), BF16)])
    return pl.pallas_call(
        functools.partial(_inproj_kernel, d=d, kv_only=kv_only),
        grid=(b, l // tm),
        in_specs=[
            tok(d),
            _mod_spec(d, mod_row),
            _const_spec((d, in_w)),
            tab, tab, tab,
            _const_spec((1, ATTN_W)), _const_spec((1, KV_W)), _const_spec((ATTN_W, ATTN_W)),
            _const_spec((N_GROUPS, CHUNK, CHUNK)), _const_spec((N_GROUPS, CHUNK, GROUP_W)),
            _const_spec((GROUP_W, 2 * GROUP_W)),
        ],
        out_specs=out_specs,
        out_shape=out_shape,
        compiler_params=_cparams(("parallel", "parallel"), fuse_inputs=(12, (2,))),
        name="in_projection_kv" if kv_only else "in_projection",
    )(x, mod, w_in, cos, sin_a, sin_b, qg, kg, gmat, sgw, sgb, dftc)


def _attn_kernel(q_ref, k_ref, vt_ref, o_ref, p_scr, *, bounded):
    t = k_ref.shape[0]

    def q_transposed(hd):
        return q_ref[:, hd * HEAD_DIM:(hd + 1) * HEAD_DIM].astype(F32).T.astype(BF16)

    def weighted_values(hd, slot, l):
        kv = hd // GQA_GROUP
        o = _dot(vt_ref[kv * HEAD_DIM:(kv + 1) * HEAD_DIM, :], p_scr[slot]) * (1.0 / l)
        o_ref[:, hd * HEAD_DIM:(hd + 1) * HEAD_DIM] = o.T.astype(BF16)

    if bounded:
        def probabilities(hd, slot):
            kv = hd // GQA_GROUP
            qt = q_transposed(hd)
            lacc = None
            for j in range(t // LANES):
                rows = slice(j * LANES, (j + 1) * LANES)
                p = jnp.exp2(_dot(k_ref[rows, kv * HEAD_DIM:(kv + 1) * HEAD_DIM], qt))
                lacc = p if lacc is None else lacc + p
                p_scr[slot, rows, :] = p.astype(BF16)
            return jnp.sum(lacc, axis=0, keepdims=True)

        l_prev = None
        for hd in range(N_Q_HEADS):
            l_cur = probabilities(hd, hd % 2)
            if hd >= 1:
                weighted_values(hd - 1, (hd - 1) % 2, l_prev)
            l_prev = l_cur
        weighted_values(N_Q_HEADS - 1, (N_Q_HEADS - 1) % 2, l_prev)
        return

    rb = 256 if t % 256 == 0 else LANES
    nb = t // rb

    def scores(hd):
        kv = hd // GQA_GROUP
        return _dot(k_ref[:, kv * HEAD_DIM:(kv + 1) * HEAD_DIM], q_transposed(hd))

    def softmax(s, slot):
        macc = s[0:rb, :]
        for j in range(1, nb):
            macc = jnp.maximum(macc, s[j * rb:(j + 1) * rb, :])
        m = jnp.max(macc, axis=0, keepdims=True)
        lacc = None
        for j in range(nb):
            p = jnp.exp2(s[j * rb:(j + 1) * rb, :] - m)
            lacc = p if lacc is None else lacc + p
            p_scr[slot, j * rb:(j + 1) * rb, :] = p.astype(BF16)
        return jnp.sum(lacc, axis=0, keepdims=True)

    s_next = scores(0)
    l_prev = None
    for hd in range(N_Q_HEADS):
        s_cur = s_next
        if hd + 1 < N_Q_HEADS:
            s_next = scores(hd + 1)
        if hd >= 1:
            weighted_values(hd - 1, (hd - 1) % 2, l_prev)
        l_prev = softmax(s_cur, hd % 2)
    weighted_values(N_Q_HEADS - 1, (N_Q_HEADS - 1) % 2, l_prev)


def _attention_call(q, k, vt, *, tq, bounded):
    b, l, _ = q.shape
    t = k.shape[1]
    return pl.pallas_call(
        functools.partial(_attn_kernel, bounded=bounded),
        grid=(b, l // tq),
        in_specs=[
            pl.BlockSpec((None, tq, ATTN_W), lambda bi, i: (bi, i, 0)),
            pl.BlockSpec((None, t, KV_W), lambda bi, i: (bi, 0, 0)),
            pl.BlockSpec((None, KV_W, t), lambda bi, i: (bi, 0, 0)),
        ],
        out_specs=pl.BlockSpec((None, tq, ATTN_W), lambda bi, i: (bi, i, 0)),
        out_shape=jax.ShapeDtypeStruct((b, l, ATTN_W), BF16),
        scratch_shapes=[pltpu.VMEM((2, t, tq), BF16)],
        compiler_params=_cparams(("parallel", "parallel")),
        name="attention_bounded" if bounded else "attention",
    )(q, k, vt)


def _max_head_norm(z):
    zf = z.astype(F32).reshape(z.shape[:-1] + (z.shape[-1] // HEAD_DIM, HEAD_DIM))
    return jnp.sqrt(jnp.max(jnp.sum(zf * zf, axis=-1)))


def _attention(q, k, vt, *, tq):
    bound = _max_head_norm(q) * _max_head_norm(k) * SCORE_BOUND_MARGIN
    return lax.cond(bound <= SCORE_BOUND,
                    functools.partial(_attention_call, tq=tq, bounded=True),
                    functools.partial(_attention_call, tq=tq, bounded=False), q, k, vt)


def _dft_cos_sin(n):
    k = np.arange(n, dtype=np.float64)
    ang = 2.0 * np.pi * np.outer(k, k) / n
    return np.cos(ang), np.sin(ang)


def _fft_stage1_kernel(x_ref, w_ref, tc_ref, ts_ref, o_ref):
    l1 = x_ref.shape[0]
    half = l1 * SUBLANES
    x = x_ref[...].reshape(half, 2 * BRANCH_W)
    stacked = jnp.concatenate([x[:, :BRANCH_W], x[:, BRANCH_W:]], axis=0).astype(BF16)
    u = _dot(w_ref[...], stacked)
    tc = jnp.concatenate([tc_ref[...]] * N_GROUPS, axis=1)
    ts = jnp.concatenate([ts_ref[...]] * N_GROUPS, axis=1)
    ur, ui = u[:half], u[half:]
    o_ref[0] = (ur * tc + ui * ts).reshape(o_ref.shape[1:])
    o_ref[1] = (ui * tc - ur * ts).reshape(o_ref.shape[1:])


def _fft_stage2_kernel(z_ref, w_ref, o_ref, *, scale):
    z = z_ref[...].reshape(-1, BRANCH_W).astype(BF16)
    o_ref[...] = (_dot(w_ref[...], z) * scale).reshape(o_ref.shape)


def _fourier_latent(ftcs):
    b, s, _ = ftcs.shape
    sub, l2 = SUBLANES, FFT_L2
    l1 = s // l2
    assert l2 == sub * sub and l1 % sub == 0
    eye = np.eye(sub)
    c1, s1 = _dft_cos_sin(l1)
    c2, s2 = _dft_cos_sin(l2)
    w1 = _bf16_const(np.kron(np.block([[c1, -s1], [-s1, -c1]]), eye))
    cs2 = np.stack([c2, s2]).reshape(2, l2, sub, sub)
    w2 = _bf16_const(np.einsum("rkxy,lm->klxrmy", cs2, eye).reshape(l2 * sub, 2 * l2 * sub))
    bpos = np.arange(l2).reshape(sub, 1, sub)
    ang = 2.0 * np.pi * (np.arange(l1).reshape(1, l1, 1) * bpos) / s
    ang = np.repeat(ang.reshape(sub, l1 * sub, 1), GROUP_W, axis=2)
    tc, ts = jnp.asarray(np.cos(ang), F32), jnp.asarray(np.sin(ang), F32)
    x5 = ftcs.reshape(b, l1, sub, sub, 2 * BRANCH_W)
    z = pl.pallas_call(
        _fft_stage1_kernel,
        grid=(b, sub),
        in_specs=[
            pl.BlockSpec((None, l1, None, sub, 2 * BRANCH_W), lambda bi, i: (bi, 0, i, 0, 0)),
            _const_spec((2 * l1 * sub, 2 * l1 * sub)),
            pl.BlockSpec((None, l1 * sub, GROUP_W), lambda bi, i: (i, 0, 0)),
            pl.BlockSpec((None, l1 * sub, GROUP_W), lambda bi, i: (i, 0, 0)),
        ],
        out_specs=pl.BlockSpec((None, None, 2, l1 // sub, l2, BRANCH_W), lambda bi, i: (bi, i, 0, 0, 0, 0)),
        out_shape=jax.ShapeDtypeStruct((b, sub, 2, l1 // sub, l2, BRANCH_W), F32),
        compiler_params=_cparams(("parallel", "parallel")),
        name="fourier_stage1",
    )(x5, w1, tc, ts)
    y = pl.pallas_call(
        functools.partial(_fft_stage2_kernel, scale=1.0 / math.sqrt(s * GROUP_W)),
        grid=(b, l1 // sub),
        in_specs=[
            pl.BlockSpec((None, sub, 2, None, l2, BRANCH_W), lambda bi, i: (bi, 0, 0, i, 0, 0)),
            _const_spec((l2 * sub, 2 * l2 * sub)),
        ],
        out_specs=pl.BlockSpec((None, l2, None, sub, BRANCH_W), lambda bi, i: (bi, 0, i, 0, 0)),
        out_shape=jax.ShapeDtypeStruct((b, l2, l1 // sub, sub, BRANCH_W), F32),
        compiler_params=_cparams(("parallel", "parallel")),
        name="fourier_stage2",
    )(z, w2)
    return y.reshape(b, s, BRANCH_W)


def _fft_direct_kernel(x_ref, w_ref, o_ref, *, scale):
    x = x_ref[...]
    stacked = jnp.concatenate([x[:, :BRANCH_W], x[:, BRANCH_W:]], axis=0).astype(BF16)
    o_ref[...] = _dot(w_ref[...], stacked) * scale


def _fourier_direct(ftcs):
    b, l, _ = ftcs.shape
    c, s = _dft_cos_sin(l)
    w = _bf16_const(np.concatenate([c, -s], axis=1))
    return pl.pallas_call(
        functools.partial(_fft_direct_kernel, scale=1.0 / math.sqrt(l * GROUP_W)),
        grid=(b,),
        in_specs=[pl.BlockSpec((None, l, 2 * BRANCH_W), lambda bi: (bi, 0, 0)), _const_spec((l, 2 * l))],
        out_specs=pl.BlockSpec((None, l, BRANCH_W), lambda bi: (bi, 0, 0)),
        out_shape=jax.ShapeDtypeStruct((b, l, BRANCH_W), F32),
        compiler_params=_cparams(("parallel",)),
        name="fourier_direct",
    )(ftcs, w)


def _post_norm(x, y, gain, bias, alpha):
    return _normalize(alpha * x + y) * gain + bias


def _merge_kernel(ya_ref, ysg_ref, yft_ref, gate_ref, x_ref, m_ref, wb_ref, wo_ref, g_ref, b_ref, o_ref, *, d, alpha):
    branches = [_dot(y_ref[...].astype(BF16), wb_ref[n]) for n, y_ref in enumerate((ya_ref, ysg_ref, yft_ref))]
    acc = None
    for n, p in enumerate(branches):
        p = gate_ref[:, n * d:(n + 1) * d].astype(F32) * p
        acc = p if acc is None else acc + p
    y = _dot(acc.astype(BF16), wo_ref[...])
    o_ref[...] = _post_norm(x_ref[...], m_ref[:, 2 * d:3 * d] * y, g_ref[...], b_ref[...], alpha)


def _merge(ya, ysg, yft, gate, x, mod, mod_row, wb, wo, gain, bias, *, tm, alpha):
    b, l, d = x.shape
    tok = lambda w: pl.BlockSpec((None, tm, w), lambda bi, i: (bi, i, 0))
    return pl.pallas_call(
        functools.partial(_merge_kernel, d=d, alpha=alpha),
        grid=(b, l // tm),
        in_specs=[tok(BRANCH_W), tok(BRANCH_W), tok(BRANCH_W), tok(N_BRANCH * d), tok(d),
                  _mod_spec(d, mod_row),
                  _const_spec((N_BRANCH, BRANCH_W, d)), _const_spec((d, d)),
                  _const_spec((1, d)), _const_spec((1, d))],
        out_specs=tok(d),
        out_shape=jax.ShapeDtypeStruct((b, l, d), F32),
        compiler_params=_cparams(("parallel", "parallel"), fuse_inputs=(10, (6, 7))),
        name="merge",
    )(ya, ysg, yft, gate, x, mod, wb, wo, gain, bias)


def _swiglu(h, wg_ref, wu_ref, wd_ref):
    gate = _dot(h, wg_ref[...])
    act = (gate * _sigmoid(gate) * _dot(h, wu_ref[...])).astype(BF16)
    return _dot(act, wd_ref[...])


def _ffn_kernel(x_ref, m_ref, wg_ref, wu_ref, wd_ref, g_ref, b_ref, o_ref, *, d, alpha):
    x = x_ref[...]
    h = _ffn_input(x, m_ref, d).astype(BF16)
    f = _swiglu(h, wg_ref, wu_ref, wd_ref)
    o_ref[...] = _post_norm(x, m_ref[:, 5 * d:6 * d] * f, g_ref[...], b_ref[...], alpha)


def _dense_ffn(x, mod, mod_row, wg, wu, wd, gain, bias, *, tm, alpha):
    b, l, d = x.shape
    dff = wg.shape[1]
    tok = pl.BlockSpec((None, tm, d), lambda bi, i: (bi, i, 0))
    return pl.pallas_call(
        functools.partial(_ffn_kernel, d=d, alpha=alpha),
        grid=(b, l // tm),
        in_specs=[tok, _mod_spec(d, mod_row),
                  _const_spec((d, dff)), _const_spec((d, dff)), _const_spec((dff, d)),
                  _const_spec((1, d)), _const_spec((1, d))],
        out_specs=tok,
        out_shape=jax.ShapeDtypeStruct((b, l, d), F32),
        compiler_params=_cparams(("parallel", "parallel"), fuse_inputs=(7, (2, 3, 4))),
        name="dense_ffn",
    )(x, mod, wg, wu, wd, gain, bias)


def _ffn_input(x, m_ref, d):
    return _normalize(x) * (1.0 + m_ref[:, 4 * d:5 * d]) + m_ref[:, 3 * d:4 * d]


def _router_kernel(x_ref, m_ref, rhi_ref, rlo_ref, tri_ref, idx_ref, aux_ref, cnt_ref, *, d, n_exp):
    tm = x_ref.shape[0]
    hi, lo = _split_bf16(_ffn_input(x_ref[...], m_ref, d))
    logits = _dot_nt(rhi_ref[...], hi) + _dot_nt(rhi_ref[...], lo) + _dot_nt(rlo_ref[...], hi)
    eid = lax.broadcasted_iota(I32, (n_exp, tm), 0).astype(F32)
    v1 = jnp.max(logits, axis=0, keepdims=True)
    i1 = jnp.min(jnp.where(logits == v1, eid, float(n_exp)), axis=0, keepdims=True)
    sel1 = eid == i1
    rest = jnp.where(sel1, -jnp.inf, logits)
    v2 = jnp.max(rest, axis=0, keepdims=True)
    i2 = jnp.min(jnp.where(rest == v2, eid, float(n_exp)), axis=0, keepdims=True)
    sel2 = eid == i2
    w1 = 1.0 / (1.0 + jnp.exp(v2 - v1))
    onehot = jnp.where(sel1 | sel2, 1.0, 0.0)
    before = _dot(onehot.astype(BF16), tri_ref[...])
    idx_ref[0:1, :] = i1.astype(I32)
    idx_ref[1:2, :] = i2.astype(I32)
    aux_ref[...] = jnp.zeros_like(aux_ref)
    aux_ref[0:1, :] = w1
    aux_ref[1:2, :] = 1.0 - w1
    aux_ref[2:3, :] = jnp.sum(jnp.where(sel1, before, 0.0), axis=0, keepdims=True)
    aux_ref[3:4, :] = jnp.sum(jnp.where(sel2, before, 0.0), axis=0, keepdims=True)
    cnt_ref[...] = jnp.broadcast_to(jnp.sum(onehot, axis=1, keepdims=True), cnt_ref.shape)


def _route(x, mod, mod_row, router, *, tm):
    b, l, d = x.shape
    n_exp = router.shape[1]
    n = b * l
    rt = router.T
    rhi = rt.astype(BF16)
    rlo = (rt - rhi.astype(F32)).astype(BF16)
    tri = _bf16_const(np.triu(np.ones((tm, tm)), 1))
    per_b = l // tm
    flat = lambda bi, i: (0, bi * per_b + i)
    return pl.pallas_call(
        functools.partial(_router_kernel, d=d, n_exp=n_exp),
        grid=(b, per_b),
        in_specs=[pl.BlockSpec((None, tm, d), lambda bi, i: (bi, i, 0)),
                  _mod_spec(d, mod_row),
                  _const_spec((n_exp, d)), _const_spec((n_exp, d)), _const_spec((tm, tm))],
        out_specs=[pl.BlockSpec((2, tm), flat), pl.BlockSpec((SUBLANES, tm), flat),
                   pl.BlockSpec((None, n_exp, LANES), lambda bi, i: (bi * per_b + i, 0, 0))],
        out_shape=[jax.ShapeDtypeStruct((2, n), I32), jax.ShapeDtypeStruct((SUBLANES, n), F32),
                   jax.ShapeDtypeStruct((n // tm, n_exp, LANES), F32)],
        compiler_params=_cparams(("parallel", "parallel")),
        name="router",
    )(x, mod, rhi, rlo, tri)


def _local_slots(idx_ref, aux_ref, start_ref):
    idx = idx_ref[...]
    base = jnp.zeros(idx.shape, F32)
    for e in range(start_ref.shape[1]):
        base = jnp.where(idx == e, start_ref[0, e].astype(F32), base)
    return base + aux_ref[2:4, :]


def _chunk_copy(src_ref, src_row, dst_ref, dst_row, sem):
    return pltpu.make_async_copy(src_ref.at[pl.ds(pl.multiple_of(src_row, SUBLANES), SUBLANES)],
                                 dst_ref.at[pl.ds(pl.multiple_of(dst_row, SUBLANES), SUBLANES)], sem)


def _scatter_kernel(rows_ref, start_ref, idx_ref, aux_ref, x_ref, m_ref, init_ref, xs_ref, buf, sem, *, d):
    del init_ref
    n_rows = buf.shape[1]
    step = pl.program_id(0) * pl.num_programs(1) + pl.program_id(1)
    last = pl.num_programs(0) * pl.num_programs(1) - 1
    cur = step % 2
    h = _ffn_input(x_ref[...], m_ref, d).astype(BF16)
    slot = _local_slots(idx_ref, aux_ref, start_ref)
    row = lax.broadcasted_iota(I32, (n_rows, h.shape[0]), 0).astype(F32)
    place = jnp.where((row == slot[0:1, :]) | (row == slot[1:2, :]), 1.0, 0.0).astype(BF16)
    buf[cur] = _dot(place, h)

    def wait_all(b):
        pltpu.make_async_copy(buf.at[b], xs_ref.at[pl.ds(0, n_rows)], sem.at[b]).wait()

    @pl.when(step > 0)
    def _():
        wait_all(1 - cur)

    def issue(c, carry):
        _chunk_copy(buf.at[cur], c * SUBLANES, xs_ref, rows_ref[0, c], sem.at[cur]).start()
        return carry

    lax.fori_loop(0, n_rows // SUBLANES, issue, 0, unroll=8)

    @pl.when(step == last)
    def _():
        wait_all(cur)


def _scatter_rows(chunk_rows, seg_start, idx, aux, x, mod, mod_row, n_rows, *, tm):
    b, l, d = x.shape
    per_b = l // tm
    n_chunks, n_exp = chunk_rows.shape[-1], seg_start.shape[-1]
    tile = lambda bi, i: (bi * per_b + i, 0, 0)
    flat = lambda bi, i: (0, bi * per_b + i)
    return pl.pallas_call(
        functools.partial(_scatter_kernel, d=d),
        grid=(b, per_b),
        in_specs=[pl.BlockSpec((None, 1, n_chunks), tile, memory_space=pltpu.SMEM),
                  pl.BlockSpec((None, 1, n_exp), tile, memory_space=pltpu.SMEM),
                  pl.BlockSpec((2, tm), flat), pl.BlockSpec((SUBLANES, tm), flat),
                  pl.BlockSpec((None, tm, d), lambda bi, i: (bi, i, 0)), _mod_spec(d, mod_row),
                  pl.BlockSpec(memory_space=pl.ANY)],
        out_specs=pl.BlockSpec(memory_space=pl.ANY),
        out_shape=jax.ShapeDtypeStruct((n_rows, d), F32),
        scratch_shapes=[pltpu.VMEM((2, n_chunks * SUBLANES, d), F32), pltpu.SemaphoreType.DMA((2,))],
        input_output_aliases={6: 0},
        compiler_params=_cparams(("arbitrary", "arbitrary")),
        name="scatter_rows",
    )(chunk_rows, seg_start, idx, aux, x, mod, jnp.zeros((n_rows, d), F32))


def _expert_kernel(te_ref, nu_ref, x_ref, wg_ref, wu_ref, wd_ref, o_ref, h_scr):
    del te_ref
    i, j = pl.program_id(0), pl.program_id(1)

    @pl.when(j == 0)
    def _():
        o_ref[...] = jnp.zeros_like(o_ref)

    @pl.when(i < nu_ref[0])
    def _():
        @pl.when(j == 0)
        def _():
            h_scr[...] = x_ref[...].astype(BF16)

        o_ref[...] += _swiglu(h_scr[...], wg_ref, wu_ref, wd_ref)


def _expert_ffn(tile_expert, n_used, xs, wg, wu, wd, *, tm, tf):
    n_rows, w = xs.shape
    n_exp, d, dff = wg.shape
    assert n_rows % tm == 0 and dff % tf == 0
    live =lambda i, nu: jnp.minimum(i, nu[0] - 1)
    grid_spec = pltpu.PrefetchScalarGridSpec(
        num_scalar_prefetch=2,
        grid=(n_rows // tm, dff // tf),
        in_specs=[pl.BlockSpec((tm, w), lambda i, j, te, nu: (live(i, nu), 0)),
                  pl.BlockSpec((None, d, tf), lambda i, j, te, nu: (te[live(i, nu)], 0, jnp.where(i < nu[0], j, dff // tf - 1))),
                  pl.BlockSpec((None, d, tf), lambda i, j, te, nu: (te[live(i, nu)], 0, jnp.where(i < nu[0], j, dff // tf - 1))),
                  pl.BlockSpec((None, tf, d), lambda i, j, te, nu: (te[live(i, nu)], jnp.where(i < nu[0], j, dff // tf - 1), 0))],
        out_specs=pl.BlockSpec((tm, w), lambda i, j, te, nu: (i, 0)),
        scratch_shapes=[pltpu.VMEM((tm, d), BF16)],
    )
    return pl.pallas_call(
        _expert_kernel,
        grid_spec=grid_spec,
        out_shape=jax.ShapeDtypeStruct((n_rows, w), F32),
        compiler_params=_cparams(("arbitrary", "arbitrary")),
        name="expert_ffn",
    )(tile_expert, n_used, xs, wg, wu, wd)


def _combine_kernel(rows_ref, start_ref, idx_ref, aux_ref, ys_ref, x_ref, m_ref, g_ref, b_ref, o_ref, buf, sem,
                    *, d, alpha):
    n_rows = buf.shape[1]
    n_chunks = n_rows // SUBLANES
    step = pl.program_id(0) * pl.num_programs(1) + pl.program_id(1)
    last = pl.num_programs(0) * pl.num_programs(1) - 1
    cur = step % 2

    def fetch(tile, b):
        def issue(c, carry):
            _chunk_copy(ys_ref, rows_ref[tile * n_chunks + c], buf.at[b], c * SUBLANES, sem.at[b]).start()
            return carry
        lax.fori_loop(0, n_chunks, issue, 0, unroll=8)

    @pl.when(step == 0)
    def _():
        fetch(step, cur)

    @pl.when(step < last)
    def _():
        fetch(step + 1, 1 - cur)

    slot = _local_slots(idx_ref, aux_ref, start_ref)
    cols = jnp.concatenate([slot, aux_ref[0:2, :], jnp.zeros((SUBLANES - 4, slot.shape[1]), F32)], axis=0).T
    lane = lax.broadcasted_iota(I32, (cols.shape[0], n_rows), 1).astype(F32)
    mix = (jnp.where(lane == cols[:, 0:1], cols[:, 2:3], 0.0)
           + jnp.where(lane == cols[:, 1:2], cols[:, 3:4], 0.0)).astype(BF16)
    pltpu.make_async_copy(ys_ref.at[pl.ds(0, n_rows)], buf.at[cur], sem.at[cur]).wait()
    f = _dot(mix, buf[cur].astype(BF16))
    o_ref[...] = _post_norm(x_ref[...], m_ref[:, 5 * d:6 * d] * f, g_ref[...], b_ref[...], alpha)


def _combine(chunk_rows, seg_start, idx, aux, ys, x, mod, mod_row, gain, bias, *, tm, alpha):
    b, l, d = x.shape
    per_b = l // tm
    n_chunks, n_exp = chunk_rows.shape[-1], seg_start.shape[-1]
    tile = lambda bi, i, rows: (bi * per_b + i, 0, 0)
    flat = lambda bi, i, rows: (0, bi * per_b + i)
    tok = pl.BlockSpec((None, tm, d), lambda bi, i, rows: (bi, i, 0))
    mod_block = _mod_spec(d, mod_row)
    fixed = lambda shape: pl.BlockSpec(shape, lambda bi, i, rows: (0,) * len(shape), pipeline_mode=pl.Buffered(1))
    grid_spec = pltpu.PrefetchScalarGridSpec(
        num_scalar_prefetch=1,
        grid=(b, per_b),
        in_specs=[pl.BlockSpec((None, 1, n_exp), tile, memory_space=pltpu.SMEM),
                  pl.BlockSpec((2, tm), flat), pl.BlockSpec((SUBLANES, tm), flat),
                  pl.BlockSpec(memory_space=pl.ANY), tok,
                  pl.BlockSpec(mod_block.block_shape, lambda bi, i, rows: mod_block.index_map(bi, i)),
                  fixed((1, d)), fixed((1, d))],
        out_specs=tok,
        scratch_shapes=[pltpu.VMEM((2, n_chunks * SUBLANES, d), F32), pltpu.SemaphoreType.DMA((2,))],
    )
    return pl.pallas_call(
        functools.partial(_combine_kernel, d=d, alpha=alpha),
        grid_spec=grid_spec,
        out_shape=jax.ShapeDtypeStruct((b, l, d), F32),
        compiler_params=_cparams(("arbitrary", "arbitrary")),
        name="combine",
    )(chunk_rows.reshape(-1), seg_start, idx, aux, ys, x, mod, gain, bias)


def _moe_ffn(x, mod, mod_row, router, wg, wu, wd, gain, bias, *, tm, tm_e, tf, alpha):
    b, l, d = x.shape
    n = b * l
    n_exp = router.shape[1]
    sub, nt = SUBLANES, n // tm
    idx, aux, counts = _route(x, mod, mod_row, router, tm=tm)
    cnt = counts[:, :, 0].astype(I32)
    seg = (cnt + sub - 1) // sub * sub
    seg_start = jnp.cumsum(seg, axis=1) - seg
    tiles = (jnp.sum(seg, axis=0) + tm_e - 1) // tm_e
    tile_end = jnp.cumsum(tiles)
    base = ((tile_end - tiles) * tm_e)[None, :] + jnp.cumsum(seg, axis=0) - seg
    n_chunks = (2 * tm) // sub + n_exp
    n_main = -(-(2 * n + nt * n_exp * (sub - 1)) // tm_e) + n_exp
    n_tiles = n_main + -(-(nt * n_exp * sub) // tm_e)
    c0 = jnp.arange(n_chunks, dtype=I32) * sub
    inside = (c0[None, :, None] >= seg_start[:, None, :]) & (c0[None, :, None] < (seg_start + seg)[:, None, :])
    target = jnp.sum(jnp.where(inside, (base - seg_start)[:, None, :] + c0[None, :, None], 0), axis=2)
    spare = (c0[None, :] - jnp.sum(seg, axis=1)[:, None]) // sub
    dump = n_main * tm_e + (jnp.arange(nt, dtype=I32)[:, None] * n_exp + spare) * sub
    chunk_rows = jnp.where(jnp.any(inside, axis=2), target, dump).reshape(nt, 1, n_chunks)
    seg_start = seg_start.reshape(nt, 1, n_exp)
    tile_ids = jnp.arange(n_tiles, dtype=I32)
    tile_expert = jnp.minimum(jnp.sum((tile_ids[:, None] >= tile_end[None, :]).astype(I32), axis=1), n_exp - 1)
    n_used = tile_end[-1:].astype(I32)
    xs = _scatter_rows(chunk_rows, seg_start, idx, aux, x, mod, mod_row, n_tiles * tm_e, tm=tm)
    ys = _expert_ffn(tile_expert, n_used, xs, wg, wu, wd, tm=tm_e, tf=tf)
    return _combine(chunk_rows, seg_start, idx, aux, ys, x, mod, mod_row, gain, bias, tm=tm, alpha=alpha)


def _token_mixer_consts(q_norm, k_norm, sg_w, sg_b):
    qg = (jnp.tile(q_norm, N_Q_HEADS) * (HEAD_DIM ** -0.5 * math.log2(math.e))).reshape(1, ATTN_W)
    kg = jnp.tile(k_norm, N_KV_HEADS).reshape(1, KV_W)
    sgb = jnp.broadcast_to(sg_b[:, :, None], (N_GROUPS, CHUNK, GROUP_W))
    return qg, kg, sg_w.astype(BF16), sgb


def kernel(x, c, ctx, c_ctx, w_mod, b_mod, w_in, q_norm, k_norm, sg_w, sg_b, w_branch, w_out, ln1_g, ln1_b, ln2_g, ln2_b, ffn_w_gate, ffn_w_up, ffn_w_down, router, exp_w_gate, exp_w_up, exp_w_down):
    b, s, d = x.shape
    cl = ctx.shape[1]
    depth = w_in.shape[0]
    assert s % (GRID_W * FFT_L2) == 0 and s % CHUNK == 0 and cl % CHUNK == 0 and b + 1 <= MOD_ROWS
    alpha = (2 * depth) ** 0.25
    t_lat, t_ctx = _token_tile(s), _token_tile(cl)

    cond = jnp.zeros((MOD_ROWS, d), F32).at[:b].set(c).at[b].set(c_ctx)
    mod = _modulation(cond, w_mod, b_mod).reshape(depth, MOD_ROWS, 1, 6 * d)

    t = np.arange(s)
    lat_tables = _rope_tables(t // GRID_W, t % GRID_W)
    ctx_tables = _rope_tables(np.zeros(cl), np.zeros(cl))
    head = np.arange(ATTN_W) // HEAD_DIM
    gmat = _bf16_const(head[:, None] == head[None, :])
    cc, sc = _dft_cos_sin(GROUP_W)
    dftc = _bf16_const(np.concatenate([cc, sc], axis=1))
    row2 = lambda v: v.reshape(1, d)

    x_lat, x_ctx = x, ctx
    for l in range(depth):
        last = l == depth - 1
        m = mod[l]
        consts = _token_mixer_consts(q_norm[l], k_norm[l], sg_w[l], sg_b[l])
        w_in_l = w_in[l].astype(BF16)
        wb, wo = w_branch[l].astype(BF16), w_out[l].astype(BF16)
        proj = functools.partial(_in_projection, w_in=w_in_l, qg=consts[0], kg=consts[1], gmat=gmat,
                                 sgw=consts[2], sgb=consts[3], dftc=dftc)
        q, k, vt, ysg, ftcs, gate = proj(x_lat, m, None, tables=lat_tables, tm=t_lat, kv_only=False)
        ctx_out = proj(x_ctx, m, b, tables=ctx_tables, tm=t_ctx, kv_only=last)
        k_c, vt_c = ctx_out[:2] if last else ctx_out[1:3]
        ya = _attention(q, jnp.concatenate([k_c, k], axis=1), jnp.concatenate([vt_c, vt], axis=2), tq=t_lat)
        yft = _fourier_latent(ftcs)
        x_lat = _merge(ya, ysg, yft, gate, x_lat, m, None, wb, wo, row2(ln1_g[l]), row2(ln1_b[l]), tm=t_lat, alpha=alpha)
        if not last:
            q_c, _, _, ysg_c, ftcs_c, gate_c = ctx_out
            ya_c = _attention(q_c, k_c, vt_c, tq=t_ctx)
            x_ctx = _merge(ya_c, ysg_c, _fourier_direct(ftcs_c), gate_c, x_ctx, m, b, wb, wo,
                           row2(ln1_g[l]), row2(ln1_b[l]), tm=t_ctx, alpha=alpha)
        i = l // 2
        if l % 2 == 0:
            wg, wu, wd = ffn_w_gate[i].astype(BF16), ffn_w_up[i].astype(BF16), ffn_w_down[i].astype(BF16)
            ffn = functools.partial(_dense_ffn, wg=wg, wu=wu, wd=wd, gain=row2(ln2_g[l]), bias=row2(ln2_b[l]), alpha=alpha)
            x_lat = ffn(x_lat, m, None, tm=t_lat)
            if not last:
                x_ctx = ffn(x_ctx, m, b, tm=t_ctx)
        else:
            dff_e = exp_w_gate.shape[-1]
            moe = functools.partial(_moe_ffn, router=router[i], wg=exp_w_gate[i].astype(BF16),
                                    wu=exp_w_up[i].astype(BF16), wd=exp_w_down[i].astype(BF16),
                                    gain=row2(ln2_g[l]), bias=row2(ln2_b[l]), tm_e=TOKEN_TILE,
                                    tf=EXPERT_FF_TILE if dff_e % EXPERT_FF_TILE == 0 else dff_e, alpha=alpha)
            x_lat = moe(x_lat, m, None, tm=t_lat)
            if not last:
                x_ctx = moe(x_ctx, m, b, tm=t_ctx)
    return x_lat
```
